```python
import math, functools
import jax, jax.numpy as jnp
from jax import lax
import numpy as np

D_MODEL = 1024
BATCH = 16
SEQ = 2048
DEPTH = 1
DEC_BATCH = 32
DEC_SEQ = 8
PAST_LEN = 16384
PAGE_SIZE = 128

N_META = 16
HEAD_DIM = 64
A_WIDTH = D_MODEL // 2
A_HEADS = A_WIDTH // HEAD_DIM
B_WIDTH = D_MODEL // 2
B_HEADS = B_WIDTH // HEAD_DIM
W_LORA = 64
A_LORA = 64
G_LORA = 128
A_PROJ = 3 * A_WIDTH + W_LORA + A_LORA + G_LORA
B_PROJ = 3 * B_WIDTH + B_HEADS
P_TOTAL = A_PROJ + B_PROJ + 2 * D_MODEL
N_GROUPS = 4
EXP_PER_GROUP = 8
N_EXPERTS = N_GROUPS * EXP_PER_GROUP
TOP_K_IN_GROUP = 2
EXPERT_FF = D_MODEL // 4
Q_BLOCK = 128
RMS_EPS = 1e-6
GN_EPS = 64e-5
NEG_INF = -1e30

kernel_name = 'rwkv7_fox_gated_hier_moe_step'


def rms_norm(x, w):
    xf = x.astype(jnp.float32)
    y = xf * lax.rsqrt(jnp.mean(xf * xf, axis=-1, keepdims=True) + RMS_EPS)
    return (y * w.astype(jnp.float32)).astype(x.dtype)


def rwkv_scan(s0, r, w, k, v, kk, a):
    tm = lambda u: jnp.moveaxis(u.astype(jnp.float32), 1, 0)
    xs = (tm(r), tm(w), tm(k), tm(v), tm(kk), tm(kk * a))

    def step(s, inp):
        r_t, w_t, k_t, v_t, kk_t, b_t = inp
        sa = jnp.einsum('bhvk,bhk->bhv', s, kk_t)
        s = s * w_t[:, :, None, :] - sa[..., None] * b_t[:, :, None, :] + v_t[..., None] * k_t[:, :, None, :]
        return s, jnp.einsum('bhvk,bhk->bhv', s, r_t)

    s_fin, o = lax.scan(step, s0.astype(jnp.float32), xs)
    return s_fin, jnp.moveaxis(o, 0, 1)


def rwkv_branch(za, shift0, s0, p):
    b, t, _ = za.shape
    za_prev = jnp.concatenate([shift0[:, None, :].astype(za.dtype), za[:, :-1]], axis=1)
    zs = za + (za_prev - za) * p['mu']
    i1, i2, i3 = A_WIDTH, 2 * A_WIDTH, 3 * A_WIDTH
    i4 = i3 + W_LORA
    i5 = i4 + A_LORA
    r, k, v = zs[..., :i1], zs[..., i1:i2], zs[..., i2:i3]
    wl, al, gl = zs[..., i3:i4], zs[..., i4:i5], zs[..., i5:]
    log_w = -math.exp(-0.5) * jax.nn.sigmoid((p['w0'] + jnp.tanh(wl) @ p['w2']).astype(jnp.float32))
    a = jax.nn.sigmoid((p['a0'] + al @ p['a2']).astype(jnp.float32))
    g = (jax.nn.sigmoid(gl) @ p['g2']).astype(jnp.float32)
    heads = lambda u: u.astype(jnp.float32).reshape(b, t, A_HEADS, HEAD_DIM)
    kk = heads(k * p['k_k'])
    kk = kk * lax.rsqrt(jnp.maximum(jnp.sum(kk * kk, axis=-1, keepdims=True), 1e-24))
    k_mod = k.astype(jnp.float32) * (1.0 + (a - 1.0) * p['k_a'].astype(jnp.float32))
    rh, kh, vh, ah, wh = heads(r), heads(k_mod), heads(v), heads(a), jnp.exp(heads(log_w))
    s_fin, o = rwkv_scan(s0, rh, wh, kh, vh, kk, ah)
    mean = jnp.mean(o, axis=-1, keepdims=True)
    var = jnp.mean(jnp.square(o - mean), axis=-1, keepdims=True)
    o = (o - mean) * lax.rsqrt(var + GN_EPS)
    o = o * p['lnx_w'].astype(jnp.float32).reshape(A_HEADS, HEAD_DIM) + p['lnx_b'].astype(jnp.float32).reshape(A_HEADS, HEAD_DIM)
    bonus = jnp.sum(rh * kh * p['r_k'].astype(jnp.float32), axis=-1, keepdims=True) * vh
    out = ((o + bonus).reshape(b, t, A_WIDTH) * g).astype(za.dtype)
    return out, s_fin, za[:, -1]


def fox_features(zb, f_bias, q_norm, k_norm):
    b, t, _ = zb.shape
    heads = lambda u: u.reshape(b, t, B_HEADS, HEAD_DIM)
    q = rms_norm(heads(zb[..., :B_WIDTH]), q_norm)
    k = rms_norm(heads(zb[..., B_WIDTH:2 * B_WIDTH]), k_norm)
    v = heads(zb[..., 2 * B_WIDTH:3 * B_WIDTH])
    log_f = jax.nn.log_sigmoid((zb[..., 3 * B_WIDTH:] + f_bias).astype(jnp.float32))
    return q, k, v, log_f


def fox_attend(q, c_q, pos_q, k, v, c_k, pos_k):
    s = jnp.einsum('bqhd,bkhd->bhqk', q.astype(jnp.float32), k.astype(jnp.float32)) * (HEAD_DIM ** -0.5)
    s = s + jnp.swapaxes(c_q, 1, 2)[:, :, :, None] - jnp.swapaxes(c_k, 1, 2)[:, :, None, :]
    s = jnp.where(pos_k[None, :] <= pos_q[:, None], s, NEG_INF)
    pr = jax.nn.softmax(s, axis=-1)
    return jnp.einsum('bhqk,bkhd->bqhd', pr, v.astype(jnp.float32))


def fox_prompt(q, k, v, log_f):
    b, L = q.shape[0], q.shape[1]
    c = jnp.cumsum(log_f, axis=1)
    pos = jnp.arange(L)
    o_meta = fox_attend(q[:, :N_META], c[:, :N_META], pos[:N_META],
                        k[:, :N_META], v[:, :N_META], c[:, :N_META], pos[:N_META])
    n_blk = (L - N_META) // Q_BLOCK

    def one_block(i):
        start = N_META + i * Q_BLOCK
        qb = lax.dynamic_slice_in_dim(q, start, Q_BLOCK, axis=1)
        cb = lax.dynamic_slice_in_dim(c, start, Q_BLOCK, axis=1)
        return fox_attend(qb, cb, start + jnp.arange(Q_BLOCK), k, v, c, pos)

    o_blk = lax.map(one_block, jnp.arange(n_blk))
    o_real = jnp.moveaxis(o_blk, 0, 1).reshape(b, n_blk * Q_BLOCK, B_HEADS, HEAD_DIM)
    return jnp.concatenate([o_meta, o_real], axis=1)


def fox_sample(q, k, v, log_f, k_past, v_past, lf_past):
    past, t = k_past.shape[1], q.shape[1]
    lf_past = lf_past.astype(jnp.float32)
    r_past = lax.cumsum(lf_past, axis=1, reverse=True) - lf_past
    c_new = jnp.cumsum(log_f, axis=1)
    keys = jnp.concatenate([k_past, k.astype(k_past.dtype)], axis=1)
    vals = jnp.concatenate([v_past, v.astype(v_past.dtype)], axis=1)
    c_k = jnp.concatenate([-r_past, c_new], axis=1)
    return fox_attend(q, c_new, past + jnp.arange(t), keys, vals, c_k, jnp.arange(past + t))


def token_mixer(h, shift0, s0, attend, p):
    xn = rms_norm(h, p['ln1'])
    z = xn @ p['w_in']
    c1 = A_PROJ
    c2 = c1 + B_PROJ
    c3 = c2 + D_MODEL
    za, zb, ga, gb = z[..., :c1], z[..., c1:c2], z[..., c2:c3], z[..., c3:]
    oa, s_new, shift_new = rwkv_branch(za, shift0, s0, p)
    q, k, v, log_f = fox_features(zb, p['f_bias'], p['q_norm'], p['k_norm'])
    ob = attend(q, k, v, log_f)
    b, t, _ = h.shape
    ob = ob.reshape(b, t, B_WIDTH).astype(h.dtype)
    merged = jax.nn.sigmoid(ga) * (oa @ p['proj_a']) + jax.nn.sigmoid(gb) * (ob @ p['proj_b'])
    return (merged @ p['w_out']).astype(h.dtype), (k, v, log_f, s_new, shift_new)


def hier_moe(x, router_grp, router_exp, w1, w3, w2):
    shp = x.shape
    xt = x.reshape(-1, D_MODEL)
    n_tok = xt.shape[0]
    grp_prob = jax.nn.softmax((xt @ router_grp).astype(jnp.float32), axis=-1)
    gp, gi = lax.top_k(grp_prob, 1)
    exp_logits = (xt @ router_exp).astype(jnp.float32).reshape(n_tok, N_GROUPS, EXP_PER_GROUP)
    sel = jnp.take_along_axis(exp_logits, gi[:, :, None], axis=1)[:, 0]
    ev, ei = lax.top_k(sel, TOP_K_IN_GROUP)
    ew = jax.nn.softmax(ev, axis=-1) * gp
    eid = gi * EXP_PER_GROUP + ei
    combine = jnp.sum(jax.nn.one_hot(eid, N_EXPERTS, dtype=jnp.float32) * ew[..., None], axis=1)
    y = jnp.zeros((n_tok, D_MODEL), jnp.float32)
    for e in range(N_EXPERTS):
        hid = jax.nn.silu(xt @ w1[e]) * (xt @ w3[e])
        y = y + combine[:, e:e + 1] * (hid @ w2[e]).astype(jnp.float32)
    return y.astype(x.dtype).reshape(shp)


def setup_inputs(seed: int = 0) -> dict:
    key = jax.random.key(seed)
    keys = list(jax.random.split(key, 48))
    f32 = jnp.float32

    def nrm(shape, scale):
        return jax.random.normal(keys.pop(), shape, f32) * scale

    def gain(shape):
        return 1.0 + nrm(shape, 0.02)

    n_pages = PAST_LEN // PAGE_SIZE
    n_used = DEC_BATCH * n_pages
    n_phys = n_used + max(1, n_used // 4)
    f_center = jnp.linspace(1.0, 4.0, B_HEADS).astype(f32)
    inp = {}
    inp['x_prompt'] = nrm((BATCH, SEQ, D_MODEL), 1.0)
    inp['x_sample'] = nrm((DEC_BATCH, DEC_SEQ, D_MODEL), 1.0)
    inp['cache_k'] = nrm((DEPTH, n_phys, PAGE_SIZE, B_HEADS, HEAD_DIM), 1.0)
    inp['cache_v'] = nrm((DEPTH, n_phys, PAGE_SIZE, B_HEADS, HEAD_DIM), 1.0)
    inp['cache_logf'] = jax.nn.log_sigmoid(f_center + nrm((DEPTH, n_phys, PAGE_SIZE, B_HEADS), 1.0))
    inp['state_rwkv'] = nrm((DEPTH, DEC_BATCH, A_HEADS, HEAD_DIM, HEAD_DIM), 0.3)
    inp['state_shift'] = nrm((DEPTH, DEC_BATCH, A_PROJ), 1.0)
    inp['page_table'] = jax.random.permutation(keys.pop(), n_phys)[:n_used].reshape(DEC_BATCH, n_pages).astype(jnp.int32)
    inp['meta_tokens'] = nrm((N_META, D_MODEL), 1.0)
    inp['ln1_w'] = gain((DEPTH, D_MODEL))
    inp['w_in'] = nrm((DEPTH, D_MODEL, P_TOTAL), D_MODEL ** -0.5)
    inp['rwkv_mu'] = jax.random.uniform(keys.pop(), (DEPTH, A_PROJ), f32)
    inp['rwkv_w0'] = nrm((DEPTH, A_WIDTH), 0.5)
    inp['rwkv_w2'] = nrm((DEPTH, W_LORA, A_WIDTH), 0.5 * W_LORA ** -0.5)
    inp['rwkv_a0'] = nrm((DEPTH, A_WIDTH), 0.1)
    inp['rwkv_a2'] = nrm((DEPTH, A_LORA, A_WIDTH), 0.5 * A_LORA ** -0.5)
    inp['rwkv_g2'] = nrm((DEPTH, G_LORA, A_WIDTH), G_LORA ** -0.5)
    inp['rwkv_k_k'] = 0.85 + nrm((DEPTH, A_WIDTH), 0.05)
    inp['rwkv_k_a'] = 1.0 + nrm((DEPTH, A_WIDTH), 0.05)
    inp['rwkv_r_k'] = nrm((DEPTH, A_HEADS, HEAD_DIM), 0.1)
    inp['rwkv_lnx_w'] = gain((DEPTH, A_WIDTH))
    inp['rwkv_lnx_b'] = nrm((DEPTH, A_WIDTH), 0.02)
    inp['fox_q_norm'] = gain((DEPTH, HEAD_DIM))
    inp['fox_k_norm'] = gain((DEPTH, HEAD_DIM))
    inp['fox_f_bias'] = f_center + nrm((DEPTH, B_HEADS), 0.05)
    inp['proj_a'] = nrm((DEPTH, A_WIDTH, D_MODEL), A_WIDTH ** -0.5)
    inp['proj_b'] = nrm((DEPTH, B_WIDTH, D_MODEL), B_WIDTH ** -0.5)
    inp['w_out'] = nrm((DEPTH, D_MODEL, D_MODEL), D_MODEL ** -0.5)
    inp['ln2_w'] = gain((DEPTH, D_MODEL))
    inp['router_grp'] = nrm((DEPTH, D_MODEL, N_GROUPS), D_MODEL ** -0.5)
    inp['router_exp'] = nrm((DEPTH, D_MODEL, N_EXPERTS), D_MODEL ** -0.5)
    inp['exp_w1'] = nrm((DEPTH, N_EXPERTS, D_MODEL, EXPERT_FF), D_MODEL ** -0.5)
    inp['exp_w3'] = nrm((DEPTH, N_EXPERTS, D_MODEL, EXPERT_FF), D_MODEL ** -0.5)
    inp['exp_w2'] = nrm((DEPTH, N_EXPERTS, EXPERT_FF, D_MODEL), EXPERT_FF ** -0.5)
    inp['ln_f'] = gain((D_MODEL,))
    return inp


def reference(x_prompt, x_sample, cache_k, cache_v, cache_logf, state_rwkv, state_shift, page_table,
              meta_tokens, ln1_w, w_in, rwkv_mu, rwkv_w0, rwkv_w2, rwkv_a0, rwkv_a2, rwkv_g2,
              rwkv_k_k, rwkv_k_a, rwkv_r_k, rwkv_lnx_w, rwkv_lnx_b, fox_q_norm, fox_k_norm, fox_f_bias,
              proj_a, proj_b, w_out, ln2_w, router_grp, router_exp, exp_w1, exp_w3, exp_w2, ln_f):
    b_p = x_prompt.shape[0]
    b_s = x_sample.shape[0]
    n_pages = page_table.shape[1]
    past = n_pages * PAGE_SIZE
    meta = jnp.broadcast_to(meta_tokens[None].astype(x_prompt.dtype), (b_p, N_META, D_MODEL))
    hp = jnp.concatenate([meta, x_prompt], axis=1)
    hs = x_sample
    out_p = ([], [], [], [], [])
    out_s = ([], [], [], [], [])
    for l in range(DEPTH):
        p = {'ln1': ln1_w[l], 'w_in': w_in[l], 'mu': rwkv_mu[l], 'w0': rwkv_w0[l], 'w2': rwkv_w2[l],
             'a0': rwkv_a0[l], 'a2': rwkv_a2[l], 'g2': rwkv_g2[l], 'k_k': rwkv_k_k[l], 'k_a': rwkv_k_a[l],
             'r_k': rwkv_r_k[l], 'lnx_w': rwkv_lnx_w[l], 'lnx_b': rwkv_lnx_b[l], 'q_norm': fox_q_norm[l],
             'k_norm': fox_k_norm[l], 'f_bias': fox_f_bias[l], 'proj_a': proj_a[l], 'proj_b': proj_b[l],
             'w_out': w_out[l]}
        shift0 = jnp.zeros((b_p, A_PROJ), hp.dtype)
        s0 = jnp.zeros((b_p, A_HEADS, HEAD_DIM, HEAD_DIM), jnp.float32)
        mix_p, st_p = token_mixer(hp, shift0, s0, fox_prompt, p)
        hp = hp + mix_p
        k_past = cache_k[l][page_table].reshape(b_s, past, B_HEADS, HEAD_DIM)
        v_past = cache_v[l][page_table].reshape(b_s, past, B_HEADS, HEAD_DIM)
        lf_past = cache_logf[l][page_table].reshape(b_s, past, B_HEADS)
        attend_s = functools.partial(fox_sample, k_past=k_past, v_past=v_past, lf_past=lf_past)
        mix_s, st_s = token_mixer(hs, state_shift[l], state_rwkv[l], attend_s, p)
        hs = hs + mix_s
        for lst, arr in zip(out_p, st_p):
            lst.append(arr)
        for lst, arr in zip(out_s, st_s):
            lst.append(arr)
        if l == DEPTH - 1:
            hp = hp[:, N_META:]
        hp = hp + hier_moe(rms_norm(hp, ln2_w[l]), router_grp[l], router_exp[l], exp_w1[l], exp_w3[l], exp_w2[l])
        hs = hs + hier_moe(rms_norm(hs, ln2_w[l]), router_grp[l], router_exp[l], exp_w1[l], exp_w3[l], exp_w2[l])
    y_prompt = rms_norm(hp, ln_f)
    y_sample = rms_norm(hs, ln_f)
    k_p, v_p, lf_p, rw_p, sh_p = (jnp.stack(a, axis=0) for a in out_p)
    k_s, v_s, lf_s, rw_s, sh_s = (jnp.stack(a, axis=0) for a in out_s)
    return (y_prompt, y_sample, k_p, v_p, lf_p, rw_p, sh_p, k_s, v_s, lf_s, rw_s, sh_s)
```

```python
import math, functools
import jax, jax.numpy as jnp
from jax import lax
from jax.experimental import pallas as pl
from jax.experimental.pallas import tpu as pltpu

D_MODEL = 1024
DEPTH = 1
PAGE_SIZE = 128
N_META = 16
HEAD_DIM = 64
A_WIDTH = D_MODEL // 2
A_HEADS = A_WIDTH // HEAD_DIM
B_WIDTH = D_MODEL // 2
B_HEADS = B_WIDTH // HEAD_DIM
W_LORA = 64
A_LORA = 64
G_LORA = 128
A_PROJ = 3 * A_WIDTH + W_LORA + A_LORA + G_LORA
B_PROJ = 3 * B_WIDTH + B_HEADS
N_GROUPS = 4
EXP_PER_GROUP = 8
N_EXPERTS = N_GROUPS * EXP_PER_GROUP
TOP_K_IN_GROUP = 2
Q_BLOCK = 128
RMS_EPS = 1e-6
GN_EPS = 64e-5
NEG_INF = -1e30


def rms_norm(x, w):
    xf = x.astype(jnp.float32)
    y = xf * lax.rsqrt(jnp.mean(xf * xf, axis=-1, keepdims=True) + RMS_EPS)
    return (y * w.astype(jnp.float32)).astype(x.dtype)


def _rms_kernel(x_ref, w_ref, o_ref):
    x = x_ref[...]
    y = x * lax.rsqrt(jnp.mean(x * x, axis=-1, keepdims=True) + RMS_EPS)
    o_ref[...] = y * w_ref[...]


def rms_norm_pallas(x, w, block_rows=512):
    shp = x.shape
    xt = x.reshape(-1, shp[-1])
    n = xt.shape[0]
    br = min(block_rows, n)
    out = pl.pallas_call(
        _rms_kernel,
        grid=(n // br,),
        in_specs=[pl.BlockSpec((br, shp[-1]), lambda i: (i, 0)),
                  pl.BlockSpec((1, shp[-1]), lambda i: (0, 0))],
        out_specs=pl.BlockSpec((br, shp[-1]), lambda i: (i, 0)),
        out_shape=jax.ShapeDtypeStruct(xt.shape, jnp.float32),
    )(xt, w.reshape(1, -1))
    return out.reshape(shp)


def rwkv_scan(s0, r, w, k, v, kk, a):
    tm = lambda u: jnp.moveaxis(u.astype(jnp.float32), 1, 0)
    xs = (tm(r), tm(w), tm(k), tm(v), tm(kk), tm(kk * a))

    def step(s, inp):
        r_t, w_t, k_t, v_t, kk_t, b_t = inp
        sa = jnp.einsum('bhvk,bhk->bhv', s, kk_t)
        s = s * w_t[:, :, None, :] - sa[..., None] * b_t[:, :, None, :] + v_t[..., None] * k_t[:, :, None, :]
        return s, jnp.einsum('bhvk,bhk->bhv', s, r_t)

    s_fin, o = lax.scan(step, s0.astype(jnp.float32), xs)
    return s_fin, jnp.moveaxis(o, 0, 1)


def rwkv_branch(za, shift0, s0, p):
    b, t, _ = za.shape
    za_prev = jnp.concatenate([shift0[:, None, :].astype(za.dtype), za[:, :-1]], axis=1)
    zs = za + (za_prev - za) * p['mu']
    i1, i2, i3 = A_WIDTH, 2 * A_WIDTH, 3 * A_WIDTH
    i4 = i3 + W_LORA
    i5 = i4 + A_LORA
    r, k, v = zs[..., :i1], zs[..., i1:i2], zs[..., i2:i3]
    wl, al, gl = zs[..., i3:i4], zs[..., i4:i5], zs[..., i5:]
    log_w = -math.exp(-0.5) * jax.nn.sigmoid((p['w0'] + jnp.tanh(wl) @ p['w2']).astype(jnp.float32))
    a = jax.nn.sigmoid((p['a0'] + al @ p['a2']).astype(jnp.float32))
    g = (jax.nn.sigmoid(gl) @ p['g2']).astype(jnp.float32)
    heads = lambda u: u.astype(jnp.float32).reshape(b, t, A_HEADS, HEAD_DIM)
    kk = heads(k * p['k_k'])
    kk = kk * lax.rsqrt(jnp.maximum(jnp.sum(kk * kk, axis=-1, keepdims=True), 1e-24))
    k_mod = k.astype(jnp.float32) * (1.0 + (a - 1.0) * p['k_a'].astype(jnp.float32))
    rh, kh, vh, ah, wh = heads(r), heads(k_mod), heads(v), heads(a), jnp.exp(heads(log_w))
    s_fin, o = rwkv_scan(s0, rh, wh, kh, vh, kk, ah)
    mean = jnp.mean(o, axis=-1, keepdims=True)
    var = jnp.mean(jnp.square(o - mean), axis=-1, keepdims=True)
    o = (o - mean) * lax.rsqrt(var + GN_EPS)
    o = o * p['lnx_w'].astype(jnp.float32).reshape(A_HEADS, HEAD_DIM) + p['lnx_b'].astype(jnp.float32).reshape(A_HEADS, HEAD_DIM)
    bonus = jnp.sum(rh * kh * p['r_k'].astype(jnp.float32), axis=-1, keepdims=True) * vh
    out = ((o + bonus).reshape(b, t, A_WIDTH) * g).astype(za.dtype)
    return out, s_fin, za[:, -1]


def fox_features(zb, f_bias, q_norm, k_norm):
    b, t, _ = zb.shape
    heads = lambda u: u.reshape(b, t, B_HEADS, HEAD_DIM)
    q = rms_norm(heads(zb[..., :B_WIDTH]), q_norm)
    k = rms_norm(heads(zb[..., B_WIDTH:2 * B_WIDTH]), k_norm)
    v = heads(zb[..., 2 * B_WIDTH:3 * B_WIDTH])
    log_f = jax.nn.log_sigmoid((zb[..., 3 * B_WIDTH:] + f_bias).astype(jnp.float32))
    return q, k, v, log_f


def fox_attend(q, c_q, pos_q, k, v, c_k, pos_k):
    s = jnp.einsum('bqhd,bkhd->bhqk', q.astype(jnp.float32), k.astype(jnp.float32)) * (HEAD_DIM ** -0.5)
    s = s + jnp.swapaxes(c_q, 1, 2)[:, :, :, None] - jnp.swapaxes(c_k, 1, 2)[:, :, None, :]
    s = jnp.where(pos_k[None, :] <= pos_q[:, None], s, NEG_INF)
    pr = jax.nn.softmax(s, axis=-1)
    return jnp.einsum('bhqk,bkhd->bqhd', pr, v.astype(jnp.float32))


def fox_prompt(q, k, v, log_f):
    b, L = q.shape[0], q.shape[1]
    c = jnp.cumsum(log_f, axis=1)
    pos = jnp.arange(L)
    o_meta = fox_attend(q[:, :N_META], c[:, :N_META], pos[:N_META],
                        k[:, :N_META], v[:, :N_META], c[:, :N_META], pos[:N_META])
    n_blk = (L - N_META) // Q_BLOCK

    def one_block(i):
        start = N_META + i * Q_BLOCK
        qb = lax.dynamic_slice_in_dim(q, start, Q_BLOCK, axis=1)
        cb = lax.dynamic_slice_in_dim(c, start, Q_BLOCK, axis=1)
        return fox_attend(qb, cb, start + jnp.arange(Q_BLOCK), k, v, c, pos)

    o_blk = lax.map(one_block, jnp.arange(n_blk))
    o_real = jnp.moveaxis(o_blk, 0, 1).reshape(b, n_blk * Q_BLOCK, B_HEADS, HEAD_DIM)
    return jnp.concatenate([o_meta, o_real], axis=1)


def fox_sample(q, k, v, log_f, k_past, v_past, lf_past):
    past, t = k_past.shape[1], q.shape[1]
    lf_past = lf_past.astype(jnp.float32)
    r_past = lax.cumsum(lf_past, axis=1, reverse=True) - lf_past
    c_new = jnp.cumsum(log_f, axis=1)
    keys = jnp.concatenate([k_past, k.astype(k_past.dtype)], axis=1)
    vals = jnp.concatenate([v_past, v.astype(v_past.dtype)], axis=1)
    c_k = jnp.concatenate([-r_past, c_new], axis=1)
    return fox_attend(q, c_new, past + jnp.arange(t), keys, vals, c_k, jnp.arange(past + t))


def token_mixer(h, shift0, s0, attend, p):
    xn = rms_norm(h, p['ln1'])
    z = xn @ p['w_in']
    c1 = A_PROJ
    c2 = c1 + B_PROJ
    c3 = c2 + D_MODEL
    za, zb, ga, gb = z[..., :c1], z[..., c1:c2], z[..., c2:c3], z[..., c3:]
    oa, s_new, shift_new = rwkv_branch(za, shift0, s0, p)
    q, k, v, log_f = fox_features(zb, p['f_bias'], p['q_norm'], p['k_norm'])
    ob = attend(q, k, v, log_f)
    b, t, _ = h.shape
    ob = ob.reshape(b, t, B_WIDTH).astype(h.dtype)
    merged = jax.nn.sigmoid(ga) * (oa @ p['proj_a']) + jax.nn.sigmoid(gb) * (ob @ p['proj_b'])
    return (merged @ p['w_out']).astype(h.dtype), (k, v, log_f, s_new, shift_new)


def hier_moe(x, router_grp, router_exp, w1, w3, w2):
    shp = x.shape
    xt = x.reshape(-1, D_MODEL)
    n_tok = xt.shape[0]
    grp_prob = jax.nn.softmax((xt @ router_grp).astype(jnp.float32), axis=-1)
    gp, gi = lax.top_k(grp_prob, 1)
    exp_logits = (xt @ router_exp).astype(jnp.float32).reshape(n_tok, N_GROUPS, EXP_PER_GROUP)
    sel = jnp.take_along_axis(exp_logits, gi[:, :, None], axis=1)[:, 0]
    ev, ei = lax.top_k(sel, TOP_K_IN_GROUP)
    ew = jax.nn.softmax(ev, axis=-1) * gp
    eid = gi * EXP_PER_GROUP + ei
    combine = jnp.sum(jax.nn.one_hot(eid, N_EXPERTS, dtype=jnp.float32) * ew[..., None], axis=1)
    y = jnp.zeros((n_tok, D_MODEL), jnp.float32)
    for e in range(N_EXPERTS):
        hid = jax.nn.silu(xt @ w1[e]) * (xt @ w3[e])
        y = y + combine[:, e:e + 1] * (hid @ w2[e]).astype(jnp.float32)
    return y.astype(x.dtype).reshape(shp)


def kernel(x_prompt, x_sample, cache_k, cache_v, cache_logf, state_rwkv, state_shift, page_table,
           meta_tokens, ln1_w, w_in, rwkv_mu, rwkv_w0, rwkv_w2, rwkv_a0, rwkv_a2, rwkv_g2,
           rwkv_k_k, rwkv_k_a, rwkv_r_k, rwkv_lnx_w, rwkv_lnx_b, fox_q_norm, fox_k_norm, fox_f_bias,
           proj_a, proj_b, w_out, ln2_w, router_grp, router_exp, exp_w1, exp_w3, exp_w2, ln_f):
    b_p = x_prompt.shape[0]
    b_s = x_sample.shape[0]
    n_pages = page_table.shape[1]
    past = n_pages * PAGE_SIZE
    meta = jnp.broadcast_to(meta_tokens[None].astype(x_prompt.dtype), (b_p, N_META, D_MODEL))
    hp = jnp.concatenate([meta, x_prompt], axis=1)
    hs = x_sample
    out_p = ([], [], [], [], [])
    out_s = ([], [], [], [], [])
    for l in range(DEPTH):
        p = {'ln1': ln1_w[l], 'w_in': w_in[l], 'mu': rwkv_mu[l], 'w0': rwkv_w0[l], 'w2': rwkv_w2[l],
             'a0': rwkv_a0[l], 'a2': rwkv_a2[l], 'g2': rwkv_g2[l], 'k_k': rwkv_k_k[l], 'k_a': rwkv_k_a[l],
             'r_k': rwkv_r_k[l], 'lnx_w': rwkv_lnx_w[l], 'lnx_b': rwkv_lnx_b[l], 'q_norm': fox_q_norm[l],
             'k_norm': fox_k_norm[l], 'f_bias': fox_f_bias[l], 'proj_a': proj_a[l], 'proj_b': proj_b[l],
             'w_out': w_out[l]}
        shift0 = jnp.zeros((b_p, A_PROJ), hp.dtype)
        s0 = jnp.zeros((b_p, A_HEADS, HEAD_DIM, HEAD_DIM), jnp.float32)
        mix_p, st_p = token_mixer(hp, shift0, s0, fox_prompt, p)
        hp = hp + mix_p
        k_past = cache_k[l][page_table].reshape(b_s, past, B_HEADS, HEAD_DIM)
        v_past = cache_v[l][page_table].reshape(b_s, past, B_HEADS, HEAD_DIM)
        lf_past = cache_logf[l][page_table].reshape(b_s, past, B_HEADS)
        attend_s = functools.partial(fox_sample, k_past=k_past, v_past=v_past, lf_past=lf_past)
        mix_s, st_s = token_mixer(hs, state_shift[l], state_rwkv[l], attend_s, p)
        hs = hs + mix_s
        for lst, arr in zip(out_p, st_p):
            lst.append(arr)
        for lst, arr in zip(out_s, st_s):
            lst.append(arr)
        if l == DEPTH - 1:
            hp = hp[:, N_META:]
        hp = hp + hier_moe(rms_norm(hp, ln2_w[l]), router_grp[l], router_exp[l], exp_w1[l], exp_w3[l], exp_w2[l])
        hs = hs + hier_moe(rms_norm(hs, ln2_w[l]), router_grp[l], router_exp[l], exp_w1[l], exp_w3[l], exp_w2[l])
    y_prompt = rms_norm_pallas(hp, ln_f)
    y_sample = rms_norm_pallas(hs, ln_f)
    k_p, v_p, lf_p, rw_p, sh_p = (jnp.stack(a, axis=0) for a in out_p)
    k_s, v_s, lf_s, rw_s, sh_s = (jnp.stack(a, axis=0) for a in out_s)
    return (y_prompt, y_sample, k_p, v_p, lf_p, rw_p, sh_p, k_s, v_s, lf_s, rw_s, sh_s)
```

```python
import functools
import math

import jax
import jax.numpy as jnp
from jax import lax
from jax.experimental import pallas as pl
from jax.experimental.pallas import tpu as pltpu

D_MODEL = 1024
PAGE_SIZE = 128
N_META = 16
HEAD_DIM = 64
A_WIDTH = D_MODEL // 2
A_HEADS = A_WIDTH // HEAD_DIM
B_WIDTH = D_MODEL // 2
B_HEADS = B_WIDTH // HEAD_DIM
W_LORA = 64
A_LORA = 64
G_LORA = 128
A_PROJ = 3 * A_WIDTH + W_LORA + A_LORA + G_LORA
B_PROJ = 3 * B_WIDTH + B_HEADS
N_GROUPS = 4
EXP_PER_GROUP = 8
N_EXPERTS = N_GROUPS * EXP_PER_GROUP
EXPERT_FF = D_MODEL // 4
RMS_EPS = 1e-6
GN_EPS = 64e-5
NEG_INF = -1e30

LANES = 128
ROW_TILE = 256
ATT_BLOCK = 128
CHUNK = 64
PAGES_PER_STEP = 8
MOE_TILE = 1024
VMEM_LIMIT = 56 * 1024 * 1024

F32 = jnp.float32
BF16 = jnp.bfloat16
NN = (((1,), (0,)), ((), ()))
NT = (((1,), (1,)), ((), ()))
TN = (((0,), (0,)), ((), ()))


def _mm(a, b, dims=NN, exact=False):
    if exact:
        return lax.dot_general(a.astype(F32), b.astype(F32), dims,
                               precision=lax.Precision.HIGHEST, preferred_element_type=F32)
    return lax.dot_general(a.astype(BF16), b.astype(BF16), dims, preferred_element_type=F32)


def _sigmoid(x):
    return 1.0 / (1.0 + jnp.exp(-x))


def _params(sem):
    return pltpu.CompilerParams(dimension_semantics=sem, vmem_limit_bytes=VMEM_LIMIT)


def _full(shape):
    n = len(shape)
    return pl.BlockSpec(shape, lambda *_: (0,) * n)


def _proj_kernel(x_ref, ln_ref, wa_ref, wqkv_ref, wf_ref, wg_ref, fb_ref, qn_ref, kn_ref, gm_ref,
                 za_ref, q_ref, k_ref, v_ref, lf_ref, sg_ref, *, exact):
    x = x_ref[...]
    xn = x * lax.rsqrt(jnp.mean(x * x, axis=-1, keepdims=True) + RMS_EPS) * ln_ref[...]
    xm = xn if exact else xn.astype(BF16)
    za_ref[...] = _mm(xm, wa_ref[...], exact=exact)
    zqkv = _mm(xm, wqkv_ref[...], exact=exact)
    gm = gm_ref[...]

    def head_norm(z, w):
        ms = _mm(z * z, gm, exact=True)
        return z * lax.rsqrt(ms + RMS_EPS) * w

    q_ref[...] = (head_norm(zqkv[:, :B_WIDTH], qn_ref[...]) * (HEAD_DIM ** -0.5)).astype(q_ref.dtype)
    k_ref[...] = head_norm(zqkv[:, B_WIDTH:2 * B_WIDTH], kn_ref[...])
    v_ref[...] = zqkv[:, 2 * B_WIDTH:]
    zf = _mm(xm, wf_ref[...], exact=exact) + fb_ref[...]
    lf_ref[...] = jnp.minimum(zf, 0.0) - jnp.log(1.0 + jnp.exp(-jnp.abs(zf)))
    sg_ref[...] = _sigmoid(_mm(xm, wg_ref[...], exact=exact)).astype(sg_ref.dtype)


def _project(x2d, w, exact, tm):
    t = x2d.shape[0]
    wdt = F32 if exact else BF16
    act = F32 if exact else BF16
    row = lambda n: pl.BlockSpec((tm, n), lambda i: (i, 0))
    return pl.pallas_call(
        functools.partial(_proj_kernel, exact=exact),
        grid=(t // tm,),
        in_specs=[row(D_MODEL), _full((1, D_MODEL)), _full((D_MODEL, A_PROJ)), _full((D_MODEL, 3 * B_WIDTH)),
                  _full((D_MODEL, LANES)), _full((D_MODEL, 2 * D_MODEL)), _full((1, LANES)),
                  _full((1, B_WIDTH)), _full((1, B_WIDTH)), _full((B_WIDTH, B_WIDTH))],
        out_specs=[row(A_PROJ), row(B_WIDTH), row(B_WIDTH), row(B_WIDTH), row(LANES), row(2 * D_MODEL)],
        out_shape=[jax.ShapeDtypeStruct((t, A_PROJ), F32), jax.ShapeDtypeStruct((t, B_WIDTH), act),
                   jax.ShapeDtypeStruct((t, B_WIDTH), F32), jax.ShapeDtypeStruct((t, B_WIDTH), F32),
                   jax.ShapeDtypeStruct((t, LANES), F32), jax.ShapeDtypeStruct((t, 2 * D_MODEL), act)],
        compiler_params=_params(("arbitrary",)),
        name="proj_in",
    )(x2d, w['ln1'], w['wa'].astype(wdt), w['wqkv'].astype(wdt), w['wf'].astype(wdt), w['wg'].astype(wdt),
      w['fb'], w['qn'], w['kn'], w['gmean'])


def _cumsum_kernel(lf_ref, c_ref, ct_ref, *, nblk):
    r = lax.broadcasted_iota(jnp.int32, (ATT_BLOCK, ATT_BLOCK), 0)
    c = lax.broadcasted_iota(jnp.int32, (ATT_BLOCK, ATT_BLOCK), 1)
    tri = (c <= r).astype(F32)
    carry = jnp.zeros((1, LANES), F32)
    for i in range(nblk):
        x = lf_ref[0, i * ATT_BLOCK:(i + 1) * ATT_BLOCK, :]
        cs = _mm(tri, x, exact=True) + carry
        c_ref[0, i * ATT_BLOCK:(i + 1) * ATT_BLOCK, :] = cs
        ct_ref[0, i] = cs.T[:B_HEADS, :]
        carry = cs[ATT_BLOCK - 1:ATT_BLOCK, :]


def _cumsum(lf3):
    b, l, _ = lf3.shape
    nblk = l // ATT_BLOCK
    return pl.pallas_call(
        functools.partial(_cumsum_kernel, nblk=nblk),
        grid=(b,),
        in_specs=[pl.BlockSpec((1, l, LANES), lambda i: (i, 0, 0))],
        out_specs=[pl.BlockSpec((1, l, LANES), lambda i: (i, 0, 0)),
                   pl.BlockSpec((1, nblk, B_HEADS, ATT_BLOCK), lambda i: (i, 0, 0, 0))],
        out_shape=[jax.ShapeDtypeStruct((b, l, LANES), F32),
                   jax.ShapeDtypeStruct((b, nblk, B_HEADS, ATT_BLOCK), F32)],
        compiler_params=_params(("arbitrary",)),
        name="logf_cumsum",
    )(lf3)


def _attn_kernel(q_ref, k_ref, v_ref, c_ref, ct_ref, o_ref, *, pad_front):
    hp = pl.program_id(1)
    qi = pl.program_id(2)
    first_blk = pad_front // ATT_BLOCK

    @pl.when(qi < first_blk)
    def _():
        o_ref[...] = jnp.zeros(o_ref.shape, o_ref.dtype)

    @pl.when(qi >= first_blk)
    def _():
        q = q_ref[0]
        cblk = c_ref[0]
        rows = qi * ATT_BLOCK + lax.broadcasted_iota(jnp.int32, (ATT_BLOCK, 1), 0)
        lane = lax.broadcasted_iota(jnp.int32, (1, ATT_BLOCK), 1)
        outs = []
        for i in range(2):
            h = hp * 2 + i
            lo, hi = HEAD_DIM * i, HEAD_DIM * (i + 1)
            qh = q[:, lo:hi]
            cq = jnp.sum(jnp.where(lane == h, cblk, 0.0), axis=-1, keepdims=True)

            def body(kj, carry, h=h, lo=lo, hi=hi, qh=qh, cq=cq):
                m, l, acc = carry
                kt = k_ref[0, pl.ds(kj * ATT_BLOCK, ATT_BLOCK), :][:, lo:hi]
                vt = v_ref[0, pl.ds(kj * ATT_BLOCK, ATT_BLOCK), :][:, lo:hi]
                ck = ct_ref[0, kj, pl.ds(h, 1), :]
                s = _mm(qh, kt, NT) + cq - ck
                kpos = kj * ATT_BLOCK + lane
                s = jnp.where((kpos <= rows) & (kpos >= pad_front), s, NEG_INF)
                m_new = jnp.maximum(m, jnp.max(s, axis=-1, keepdims=True))
                alpha = jnp.exp(m - m_new)
                p = jnp.exp(s - m_new)
                l = alpha * l + jnp.sum(p, axis=-1, keepdims=True)
                acc = alpha * acc + _mm(p, vt)
                return m_new, l, acc

            init = (jnp.full((ATT_BLOCK, 1), NEG_INF, F32), jnp.zeros((ATT_BLOCK, 1), F32),
                    jnp.zeros((ATT_BLOCK, HEAD_DIM), F32))
            _, l, acc = lax.fori_loop(first_blk, qi + 1, body, init)
            outs.append(acc / l)
        o_ref[0] = jnp.concatenate(outs, axis=-1).astype(o_ref.dtype)


def _attention_prompt(q3, k3, v3, c3, ct4, pad_front):
    b, l, _ = q3.shape
    nq = l // ATT_BLOCK
    return pl.pallas_call(
        functools.partial(_attn_kernel, pad_front=pad_front),
        grid=(b, B_HEADS // 2, nq),
        in_specs=[pl.BlockSpec((1, ATT_BLOCK, LANES), lambda i, h, j: (i, j, h)),
                  pl.BlockSpec((1, l, LANES), lambda i, h, j: (i, 0, h)),
                  pl.BlockSpec((1, l, LANES), lambda i, h, j: (i, 0, h)),
                  pl.BlockSpec((1, ATT_BLOCK, LANES), lambda i, h, j: (i, j, 0)),
                  pl.BlockSpec((1, nq, B_HEADS, ATT_BLOCK), lambda i, h, j: (i, 0, 0, 0))],
        out_specs=pl.BlockSpec((1, ATT_BLOCK, LANES), lambda i, h, j: (i, j, h)),
        out_shape=jax.ShapeDtypeStruct((b, l, B_WIDTH), BF16),
        compiler_params=_params(("arbitrary", "arbitrary", "arbitrary")),
        name="fox_prompt_attn",
    )(q3, k3, v3, c3, ct4)


def _attn_sample_kernel(pt_ref, q_ref, kn_ref, vn_ref, lfn_ref, *rest, n_steps, t_new, exact):
    pps = PAGES_PER_STEP
    kp_refs, vp_refs, lp_refs = rest[:pps], rest[pps:2 * pps], rest[2 * pps:3 * pps]
    o_ref = rest[3 * pps]
    m_ref, l_ref, acc_ref, car_ref, bq_ref = rest[3 * pps + 1:]
    j = pl.program_id(1)
    nrow = B_HEADS * t_new
    row_h = lax.broadcasted_iota(jnp.int32, (nrow, 1), 0) // t_new
    row_t = lax.broadcasted_iota(jnp.int32, (nrow, 1), 0) % t_new
    lane512 = lax.broadcasted_iota(jnp.int32, (1, B_WIDTH), 1)
    head_mask = (lane512 // HEAD_DIM == row_h).astype(F32)
    q = q_ref[0]
    qbd = jnp.broadcast_to(q[None], (B_HEADS, t_new, B_WIDTH)).reshape(nrow, B_WIDTH) * head_mask
    lane = lax.broadcasted_iota(jnp.int32, (1, LANES), 1)
    r_i = lax.broadcasted_iota(jnp.int32, (LANES, LANES), 0)
    c_i = lax.broadcasted_iota(jnp.int32, (LANES, LANES), 1)

    def to_rows(x8):
        n = x8.shape[-1]
        return jnp.broadcast_to(x8[:, None, :], (B_HEADS, t_new, n)).reshape(nrow, n)

    @pl.when(j == 0)
    def _():
        zrows = lambda u: jnp.concatenate([u, jnp.zeros((LANES - t_new, u.shape[1]), u.dtype)], axis=0)
        lfn = zrows(lfn_ref[0])
        cn = _mm((c_i <= r_i).astype(F32), lfn, exact=True)
        cn_t = cn.T[:B_HEADS, :]
        sel = (lane == row_h).astype(F32)
        cn_rows = jnp.broadcast_to(cn[None, :t_new], (B_HEADS, t_new, LANES)).reshape(nrow, LANES)
        cq = jnp.sum(cn_rows * sel, axis=-1, keepdims=True)
        bq_ref[...] = cq
        s = _mm(qbd, zrows(kn_ref[0]), NT, exact=exact) + cq - to_rows(cn_t)
        s = jnp.where(lane <= row_t, s, NEG_INF)
        m = jnp.max(s, axis=-1, keepdims=True)
        p = jnp.exp(s - m)
        m_ref[...] = m
        l_ref[...] = jnp.sum(p, axis=-1, keepdims=True)
        acc_ref[...] = _mm(p, zrows(vn_ref[0]), exact=exact)
        car_ref[...] = jnp.zeros(car_ref.shape, F32)

    cq = bq_ref[...]
    strict = (r_i > c_i).astype(F32)
    m, l, acc, car = m_ref[...], l_ref[...], acc_ref[...], car_ref[...]
    for i in range(pps):
        lf = lp_refs[i][0]
        lf_t = jnp.concatenate([lf, jnp.zeros((PAGE_SIZE, LANES - B_HEADS), F32)], axis=1).T[:B_HEADS, :]
        suf = _mm(lf_t, strict, exact=True) + car
        car = car + jnp.sum(lf_t, axis=-1, keepdims=True)
        s = _mm(qbd, kp_refs[i][0], NT, exact=exact) + cq + to_rows(suf)
        m_new = jnp.maximum(m, jnp.max(s, axis=-1, keepdims=True))
        alpha = jnp.exp(m - m_new)
        p = jnp.exp(s - m_new)
        l = alpha * l + jnp.sum(p, axis=-1, keepdims=True)
        acc = alpha * acc + _mm(p, vp_refs[i][0], exact=exact)
        m = m_new
    m_ref[...], l_ref[...], acc_ref[...], car_ref[...] = m, l, acc, car

    @pl.when(j == n_steps - 1)
    def _():
        o = (acc / l) * head_mask
        o_ref[0] = jnp.sum(o.reshape(B_HEADS, t_new, B_WIDTH), axis=0)


def _attention_sample(q3, k3, v3, lf3, cache_k, cache_v, cache_lf, page_table, exact):
    b, t_new, _ = q3.shape
    n_pages = page_table.shape[1]
    pps = PAGES_PER_STEP
    n_steps = n_pages // pps
    nrow = B_HEADS * t_new

    def page_map(i):
        return lambda bi, j, pt: (pt[bi, n_pages - 1 - (j * pps + i)], 0, 0)

    new = lambda n: pl.BlockSpec((1, t_new, n), lambda bi, j, pt: (bi, 0, 0))
    in_specs = [new(B_WIDTH), new(B_WIDTH), new(B_WIDTH), new(LANES)]
    in_specs += [pl.BlockSpec((1, PAGE_SIZE, B_WIDTH), page_map(i)) for i in range(pps)]
    in_specs += [pl.BlockSpec((1, PAGE_SIZE, B_WIDTH), page_map(i)) for i in range(pps)]
    in_specs += [pl.BlockSpec((1, PAGE_SIZE, B_HEADS), page_map(i)) for i in range(pps)]
    grid_spec = pltpu.PrefetchScalarGridSpec(
        num_scalar_prefetch=1, grid=(b, n_steps), in_specs=in_specs,
        out_specs=pl.BlockSpec((1, t_new, B_WIDTH), lambda bi, j, pt: (bi, 0, 0)),
        scratch_shapes=[pltpu.VMEM((nrow, 1), F32), pltpu.VMEM((nrow, 1), F32), pltpu.VMEM((nrow, B_WIDTH), F32),
                        pltpu.VMEM((B_HEADS, 1), F32), pltpu.VMEM((nrow, 1), F32)])
    return pl.pallas_call(
        functools.partial(_attn_sample_kernel, n_steps=n_steps, t_new=t_new, exact=exact),
        grid_spec=grid_spec,
        out_shape=jax.ShapeDtypeStruct((b, t_new, B_WIDTH), F32),
        compiler_params=_params(("arbitrary", "arbitrary")),
        name="fox_sample_attn",
    )(page_table, q3, k3, v3, lf3, *([cache_k] * pps), *([cache_v] * pps), *([cache_lf] * pps))


def _rwkv_kernel(za_ref, s0_ref, sh0_ref, mu_ref, w0_ref, a0_ref, kk_ref, ka_ref, rk_ref, lw_ref, lb_ref,
                 w2_ref, a2_ref, g2_ref, gs_ref, oa_ref, sout_ref, prev_ref, st_ref, *, valid_rows, exact):
    c = za_ref.shape[1]
    j = pl.program_id(1)

    @pl.when(j == 0)
    def _():
        prev_ref[...] = sh0_ref[0]
        st_ref[...] = s0_ref[0]

    za = za_ref[0]
    row = lax.broadcasted_iota(jnp.int32, (c, 1), 0)
    zprev = jnp.where(row == 0, prev_ref[...], pltpu.roll(za, 1, 0))
    prev_ref[...] = za[c - 1:c, :]
    zs = za + (zprev - za) * mu_ref[...]
    i1, i2, i3 = A_WIDTH, 2 * A_WIDTH, 3 * A_WIDTH
    i4 = i3 + W_LORA
    i5 = i4 + A_LORA
    r, k, v = zs[:, :i1], zs[:, i1:i2], zs[:, i2:i3]
    wl, al, gl = zs[:, i3:i4], zs[:, i4:i5], zs[:, i5:]
    logw = -math.exp(-0.5) * _sigmoid(w0_ref[...] + _mm(jnp.tanh(wl), w2_ref[...], exact=exact))
    a = _sigmoid(a0_ref[...] + _mm(al, a2_ref[...], exact=exact))
    g = _mm(_sigmoid(gl), g2_ref[...], exact=exact)
    kk = k * kk_ref[...]
    kk = kk * lax.rsqrt(jnp.maximum(_mm(kk * kk, gs_ref[...], exact=True), 1e-24))
    kmod = k * (1.0 + (a - 1.0) * ka_ref[...])
    bb = kk * a
    if valid_rows < c:
        live = row < valid_rows
        logw = jnp.where(live, logw, 0.0)
        v, kmod, bb, kk = (jnp.where(live, u, 0.0) for u in (v, kmod, bb, kk))

    ri = lax.broadcasted_iota(jnp.int32, (c, c), 0)
    ci = lax.broadcasted_iota(jnp.int32, (c, c), 1)
    incl = (ci <= ri).astype(F32)
    strict = (ci < ri).astype(F32)
    eye = (ci == ri).astype(F32)
    gcum = _mm(incl, logw, exact=True)
    glast = gcum[c - 1:c, :]
    p_in, p_out = jnp.exp(gcum), jnp.exp(-gcum)
    tail = jnp.exp(glast - gcum)
    kt = kk * jnp.exp(gcum - logw)
    bt = bb * p_out
    kkt = kmod * p_out
    rt = r * p_in
    bh = bb * tail
    kh = kmod * tail
    pc = jnp.exp(glast)

    outs = []
    for h in range(A_HEADS):
        sl = slice(h * HEAD_DIM, (h + 1) * HEAD_DIM)
        m0 = st_ref[h]
        lhs = jnp.concatenate([kt[:, sl], rt[:, sl]], axis=0)
        ab = _mm(lhs, bt[:, sl], NT, exact=exact)
        ak = _mm(lhs, kkt[:, sl], NT, exact=exact)
        a_ab, a_rb = ab[:c] * strict, ab[c:] * incl
        a_ak, a_rk = ak[:c] * strict, ak[c:] * incl
        pw = -a_ab
        x = eye + pw
        n = 1
        while 2 * n < c:
            pw = _mm(pw, pw, exact=exact)
            x = x + _mm(x, pw, exact=exact)
            n *= 2
        lm = _mm(lhs, m0, NT, exact=exact)
        vh = v[:, sl]
        u = _mm(x, lm[:c] + _mm(a_ak, vh, exact=exact), exact=exact)
        o = lm[c:] - _mm(a_rb, u, exact=exact) + _mm(a_rk, vh, exact=exact)
        st_ref[h] = m0 * pc[:, sl] + _mm(vh, kh[:, sl], TN, exact=exact) - _mm(u, bh[:, sl], TN, exact=exact)
        mean = jnp.mean(o, axis=-1, keepdims=True)
        var = jnp.mean(jnp.square(o - mean), axis=-1, keepdims=True)
        o = (o - mean) * lax.rsqrt(var + GN_EPS) * lw_ref[:, sl] + lb_ref[:, sl]
        bonus = jnp.sum(r[:, sl] * kmod[:, sl] * rk_ref[:, sl], axis=-1, keepdims=True) * vh
        outs.append((o + bonus) * g[:, sl])
    oa_ref[0] = jnp.concatenate(outs, axis=-1).astype(oa_ref.dtype)

    @pl.when(j == pl.num_programs(1) - 1)
    def _():
        sout_ref[0] = st_ref[...]


def _rwkv(za3, s0, shift0, w, valid_rows, exact):
    b, l, _ = za3.shape
    c = CHUNK
    vec = lambda n: _full((1, n))
    wdt = F32 if exact else BF16
    return pl.pallas_call(
        functools.partial(_rwkv_kernel, valid_rows=valid_rows, exact=exact),
        grid=(b, l // c),
        in_specs=[pl.BlockSpec((1, c, A_PROJ), lambda i, j: (i, j, 0)),
                  pl.BlockSpec((1, A_HEADS, HEAD_DIM, HEAD_DIM), lambda i, j: (i, 0, 0, 0)),
                  pl.BlockSpec((1, 1, A_PROJ), lambda i, j: (i, 0, 0)),
                  vec(A_PROJ), vec(A_WIDTH), vec(A_WIDTH), vec(A_WIDTH), vec(A_WIDTH), vec(A_WIDTH),
                  vec(A_WIDTH), vec(A_WIDTH),
                  _full((W_LORA, A_WIDTH)), _full((A_LORA, A_WIDTH)), _full((G_LORA, A_WIDTH)),
                  _full((A_WIDTH, A_WIDTH))],
        out_specs=[pl.BlockSpec((1, c, A_WIDTH), lambda i, j: (i, j, 0)),
                   pl.BlockSpec((1, A_HEADS, HEAD_DIM, HEAD_DIM), lambda i, j: (i, 0, 0, 0))],
        out_shape=[jax.ShapeDtypeStruct((b, l, A_WIDTH), F32 if exact else BF16),
                   jax.ShapeDtypeStruct((b, A_HEADS, HEAD_DIM, HEAD_DIM), F32)],
        scratch_shapes=[pltpu.VMEM((1, A_PROJ), F32), pltpu.VMEM((A_HEADS, HEAD_DIM, HEAD_DIM), F32)],
        compiler_params=_params(("arbitrary", "arbitrary")),
        name="rwkv7_chunked",
    )(za3, s0, shift0[:, None, :], w['mu'], w['w0'], w['a0'], w['k_k'], w['k_a'], w['r_k'], w['lnx_w'], w['lnx_b'],
      w['w2'].astype(wdt), w['a2'].astype(wdt), w['g2'].astype(wdt), w['gsum'])


def _merge_kernel(oa_ref, ob_ref, sg_ref, x_ref, pa_ref, pb_ref, wo_ref, ln2_ref, wr_ref,
                  h_ref, xn_ref, lg_ref, *, exact):
    sg = sg_ref[0].astype(F32)
    ya = _mm(oa_ref[0], pa_ref[...], exact=exact)
    yb = _mm(ob_ref[0], pb_ref[...], exact=exact)
    merged = sg[:, :D_MODEL] * ya + sg[:, D_MODEL:] * yb
    h = x_ref[0] + _mm(merged, wo_ref[...], exact=exact)
    h_ref[0] = h
    xn = h * lax.rsqrt(jnp.mean(h * h, axis=-1, keepdims=True) + RMS_EPS) * ln2_ref[...]
    xn_ref[0] = xn.astype(xn_ref.dtype)
    lg_ref[0] = _mm(xn, wr_ref[...], exact=exact)


def _merge(oa3, ob3, sg3, x3, w, row_off, exact, tm):
    b, s, _ = x3.shape
    wdt = F32 if exact else BF16
    blk = lambda n, off: pl.BlockSpec((1, tm, n), lambda i, j: (i, j + off, 0))
    return pl.pallas_call(
        functools.partial(_merge_kernel, exact=exact),
        grid=(b, s // tm),
        in_specs=[blk(A_WIDTH, row_off), blk(B_WIDTH, row_off), blk(2 * D_MODEL, row_off), blk(D_MODEL, 0),
                  _full((A_WIDTH, D_MODEL)), _full((B_WIDTH, D_MODEL)), _full((D_MODEL, D_MODEL)),
                  _full((1, D_MODEL)), _full((D_MODEL, LANES))],
        out_specs=[blk(D_MODEL, 0), blk(D_MODEL, 0), blk(LANES, 0)],
        out_shape=[jax.ShapeDtypeStruct((b, s, D_MODEL), F32),
                   jax.ShapeDtypeStruct((b, s, D_MODEL), F32 if exact else BF16),
                   jax.ShapeDtypeStruct((b, s, LANES), F32)],
        compiler_params=_params(("arbitrary", "arbitrary")),
        name="merge_out",
    )(oa3, ob3, sg3, x3, w['proj_a'].astype(wdt), w['proj_b'].astype(wdt), w['w_out'].astype(wdt),
      w['ln2'], w['wr'].astype(wdt))


def _route(lg):
    lane = lax.broadcasted_iota(jnp.int32, lg.shape, 1)
    big = jnp.int32(1 << 20)
    is_g = (lane >= N_EXPERTS) & (lane < N_EXPERTS + N_GROUPS)
    gl = jnp.where(is_g, lg, NEG_INF)
    gmax = jnp.max(gl, axis=-1, keepdims=True)
    gp = 1.0 / jnp.sum(jnp.where(is_g, jnp.exp(gl - gmax), 0.0), axis=-1, keepdims=True)
    gi = jnp.min(jnp.where(is_g & (gl == gmax), lane, big), axis=-1, keepdims=True) - N_EXPERTS
    in_g = (lane < N_EXPERTS) & (lane // EXP_PER_GROUP == gi)
    sel = jnp.where(in_g, lg, NEG_INF)
    v1 = jnp.max(sel, axis=-1, keepdims=True)
    i1 = jnp.min(jnp.where(in_g & (sel == v1), lane, big), axis=-1, keepdims=True)
    sel2 = jnp.where(lane == i1, NEG_INF, sel)
    v2 = jnp.max(sel2, axis=-1, keepdims=True)
    i2 = jnp.min(jnp.where(in_g & (lane != i1) & (sel2 == v2), lane, big), axis=-1, keepdims=True)
    e2 = jnp.exp(v2 - v1)
    w1 = gp / (1.0 + e2)
    w2 = gp * e2 / (1.0 + e2)
    return jnp.where(lane == i1, w1, 0.0) + jnp.where(lane == i2, w2, 0.0)


def _moe_kernel(xn_ref, lg_ref, h_ref, w1_ref, w3_ref, w2_ref, lnf_ref, y_ref, acc_ref, cmb_ref, *, exact):
    e = pl.program_id(1)

    @pl.when(e == 0)
    def _():
        acc_ref[...] = jnp.zeros(acc_ref.shape, F32)
        cmb_ref[...] = _route(lg_ref[...])

    x = xn_ref[...]
    a = _mm(x, w1_ref[0], exact=exact)
    hid = a * _sigmoid(a) * _mm(x, w3_ref[0], exact=exact)
    lane = lax.broadcasted_iota(jnp.int32, (1, LANES), 1)
    ce = jnp.sum(jnp.where(lane == e, cmb_ref[...], 0.0), axis=-1, keepdims=True)
    acc_ref[...] += _mm(hid * ce, w2_ref[0], exact=exact)

    @pl.when(e == N_EXPERTS - 1)
    def _():
        y = h_ref[...] + acc_ref[...]
        y_ref[...] = y * lax.rsqrt(jnp.mean(y * y, axis=-1, keepdims=True) + RMS_EPS) * lnf_ref[...]


def _moe(xn2d, lg2d, h2d, w, exact, tm):
    t = xn2d.shape[0]
    wdt = F32 if exact else BF16
    row = lambda n: pl.BlockSpec((tm, n), lambda i, e: (i, 0))
    return pl.pallas_call(
        functools.partial(_moe_kernel, exact=exact),
        grid=(t // tm, N_EXPERTS),
        in_specs=[row(D_MODEL), row(LANES), row(D_MODEL),
                  pl.BlockSpec((1, D_MODEL, EXPERT_FF), lambda i, e: (e, 0, 0)),
                  pl.BlockSpec((1, D_MODEL, EXPERT_FF), lambda i, e: (e, 0, 0)),
                  pl.BlockSpec((1, EXPERT_FF, D_MODEL), lambda i, e: (e, 0, 0)),
                  pl.BlockSpec((1, D_MODEL), lambda i, e: (0, 0))],
        out_specs=row(D_MODEL),
        out_shape=jax.ShapeDtypeStruct((t, D_MODEL), F32),
        scratch_shapes=[pltpu.VMEM((tm, D_MODEL), F32), pltpu.VMEM((tm, LANES), F32)],
        compiler_params=_params(("arbitrary", "arbitrary")),
        name="hier_moe",
    )(xn2d, lg2d, h2d, w['w1'].astype(wdt), w['w3'].astype(wdt), w['w2'].astype(wdt), w['lnf'])


def _layer_weights(l, ln1_w, w_in, rwkv_mu, rwkv_w0, rwkv_w2, rwkv_a0, rwkv_a2, rwkv_g2, rwkv_k_k, rwkv_k_a,
                   rwkv_r_k, rwkv_lnx_w, rwkv_lnx_b, fox_q_norm, fox_k_norm, fox_f_bias, proj_a, proj_b, w_out,
                   ln2_w, router_grp, router_exp, exp_w1, exp_w3, exp_w2, ln_f):
    wi = w_in[l]
    c1 = A_PROJ
    c2 = c1 + 3 * B_WIDTH
    c3 = c2 + B_HEADS
    head_of = jnp.arange(B_WIDTH) // HEAD_DIM
    same_head = (head_of[:, None] == head_of[None, :]).astype(F32)
    row = lambda u: u.reshape(1, -1).astype(F32)
    return dict(
        ln1=row(ln1_w[l]), wa=wi[:, :c1], wqkv=wi[:, c1:c2],
        wf=jnp.pad(wi[:, c2:c3], ((0, 0), (0, LANES - B_HEADS))), wg=wi[:, c3:],
        fb=jnp.pad(row(fox_f_bias[l]), ((0, 0), (0, LANES - B_HEADS))),
        qn=row(jnp.tile(fox_q_norm[l], B_HEADS)), kn=row(jnp.tile(fox_k_norm[l], B_HEADS)),
        gmean=same_head / HEAD_DIM, gsum=same_head,
        mu=row(rwkv_mu[l]), w0=row(rwkv_w0[l]), a0=row(rwkv_a0[l]), k_k=row(rwkv_k_k[l]), k_a=row(rwkv_k_a[l]),
        r_k=row(rwkv_r_k[l]), lnx_w=row(rwkv_lnx_w[l]), lnx_b=row(rwkv_lnx_b[l]),
        w2=rwkv_w2[l], a2=rwkv_a2[l], g2=rwkv_g2[l],
        proj_a=proj_a[l], proj_b=proj_b[l], w_out=w_out[l], ln2=row(ln2_w[l]),
        wr=jnp.pad(jnp.concatenate([router_exp[l], router_grp[l]], axis=1),
                   ((0, 0), (0, LANES - N_EXPERTS - N_GROUPS))),
        w1=exp_w1[l], w3=exp_w3[l], w2e=exp_w2[l], lnf=row(ln_f))


def kernel(x_prompt, x_sample, cache_k, cache_v, cache_logf, state_rwkv, state_shift, page_table,
           meta_tokens, ln1_w, w_in, rwkv_mu, rwkv_w0, rwkv_w2, rwkv_a0, rwkv_a2, rwkv_g2,
           rwkv_k_k, rwkv_k_a, rwkv_r_k, rwkv_lnx_w, rwkv_lnx_b, fox_q_norm, fox_k_norm, fox_f_bias,
           proj_a, proj_b, w_out, ln2_w, router_grp, router_exp, exp_w1, exp_w3, exp_w2, ln_f):
    depth = w_in.shape[0]
    assert depth == 1, "single trunk layer"
    b_p, seq, _ = x_prompt.shape
    b_s, t_new, _ = x_sample.shape
    assert seq % ROW_TILE == 0 and t_new <= CHUNK and page_table.shape[1] % PAGES_PER_STEP == 0
    w = _layer_weights(0, ln1_w, w_in, rwkv_mu, rwkv_w0, rwkv_w2, rwkv_a0, rwkv_a2, rwkv_g2, rwkv_k_k, rwkv_k_a,
                       rwkv_r_k, rwkv_lnx_w, rwkv_lnx_b, fox_q_norm, fox_k_norm, fox_f_bias, proj_a, proj_b,
                       w_out, ln2_w, router_grp, router_exp, exp_w1, exp_w3, exp_w2, ln_f)
    wm = dict(w, w2=w['w2e'])

    pad_front = ROW_TILE - N_META
    l_pad = ROW_TILE + seq
    l_real = N_META + seq
    head = jnp.concatenate([jnp.zeros((pad_front, D_MODEL), F32), meta_tokens.astype(F32)], axis=0)
    hp = jnp.concatenate([jnp.broadcast_to(head[None], (b_p, ROW_TILE, D_MODEL)), x_prompt], axis=1)
    za, q, k, v, lf, sg = _project(hp.reshape(b_p * l_pad, D_MODEL), w, False, ROW_TILE)
    r3 = lambda u: u.reshape(b_p, l_pad, u.shape[-1])
    c3, ct4 = _cumsum(r3(lf))
    ob = _attention_prompt(r3(q), r3(k), r3(v), c3, ct4, pad_front)
    oa, s_p = _rwkv(r3(za), jnp.zeros((b_p, A_HEADS, HEAD_DIM, HEAD_DIM), F32), jnp.zeros((b_p, A_PROJ), F32),
                    w, CHUNK, False)
    h_p, xn_p, lg_p = _merge(oa, ob, r3(sg), x_prompt, w, 1, False, ROW_TILE)
    flat = lambda u: u.reshape(b_p * seq, u.shape[-1])
    y_prompt = _moe(flat(xn_p), flat(lg_p), flat(h_p), wm, False, math.gcd(b_p * seq, MOE_TILE))
    y_prompt = y_prompt.reshape(b_p, seq, D_MODEL)
    k_p = r3(k)[:, pad_front:].reshape(1, b_p, l_real, B_HEADS, HEAD_DIM)
    v_p = r3(v)[:, pad_front:].reshape(1, b_p, l_real, B_HEADS, HEAD_DIM)
    lf_p = r3(lf)[:, pad_front:, :B_HEADS][None]
    sh_p = r3(za)[:, l_pad - 1][None]

    n_s = b_s * t_new
    za_s, q_s, k_s, v_s, lf_s, sg_s = _project(x_sample.reshape(n_s, D_MODEL), w, True, n_s)
    s3 = lambda u: u.reshape(b_s, t_new, u.shape[-1])
    ob_s = _attention_sample(s3(q_s), s3(k_s), s3(v_s), s3(lf_s),
                             cache_k[0].reshape(-1, PAGE_SIZE, B_WIDTH), cache_v[0].reshape(-1, PAGE_SIZE, B_WIDTH),
                             cache_logf[0], page_table, False)
    za_pad = jnp.pad(s3(za_s), ((0, 0), (0, CHUNK - t_new), (0, 0)))
    oa_s, s_s = _rwkv(za_pad, state_rwkv[0], state_shift[0], w, t_new, True)
    one = lambda u: u.reshape(1, n_s, u.shape[-1])
    h_s, xn_s, lg_s = _merge(one(oa_s[:, :t_new]), one(ob_s), one(sg_s), one(x_sample), w, 0, True, n_s)
    y_sample = _moe(xn_s[0], lg_s[0], h_s[0], wm, True, n_s).reshape(b_s, t_new, D_MODEL)
    k_sn = s3(k_s).reshape(1, b_s, t_new, B_HEADS, HEAD_DIM)
    v_sn = s3(v_s).reshape(1, b_s, t_new, B_HEADS, HEAD_DIM)
    lf_sn = s3(lf_s)[:, :, :B_HEADS][None]
    sh_s = s3(za_s)[:, t_new - 1][None]
    return (y_prompt, y_sample, k_p, v_p, lf_p, s_p[None], sh_p, k_sn, v_sn, lf_sn, s_s[None], sh_s)
```

```python
import functools
import math

import jax
import jax.numpy as jnp
from jax import lax
from jax.experimental import pallas as pl
from jax.experimental.pallas import tpu as pltpu

D_MODEL = 1024
PAGE_SIZE = 128
N_META = 16
HEAD_DIM = 64
A_WIDTH = D_MODEL // 2
A_HEADS = A_WIDTH // HEAD_DIM
B_WIDTH = D_MODEL // 2
B_HEADS = B_WIDTH // HEAD_DIM
W_LORA = 64
A_LORA = 64
G_LORA = 128
A_PROJ = 3 * A_WIDTH + W_LORA + A_LORA + G_LORA
B_PROJ = 3 * B_WIDTH + B_HEADS
N_GROUPS = 4
EXP_PER_GROUP = 8
N_EXPERTS = N_GROUPS * EXP_PER_GROUP
EXPERT_FF = D_MODEL // 4
RMS_EPS = 1e-6
GN_EPS = 64e-5
NEG_INF = -1e30

LANES = 128
ROW_TILE = 256
ATT_BLOCK = 128
KEY_TILE = 256
CHUNK = 64
PAGES_PER_STEP = 16
MOE_TILE = 1024
VMEM_LIMIT = 56 * 1024 * 1024

F32 = jnp.float32
BF16 = jnp.bfloat16
NN = (((1,), (0,)), ((), ()))
NT = (((1,), (1,)), ((), ()))
TN = (((0,), (0,)), ((), ()))


def _mm(a, b, dims=NN, exact=False):
    if exact:
        return lax.dot_general(a.astype(F32), b.astype(F32), dims,
                               precision=lax.Precision.HIGHEST, preferred_element_type=F32)
    return lax.dot_general(a.astype(BF16), b.astype(BF16), dims, preferred_element_type=F32)


def _sigmoid(x):
    return 1.0 / (1.0 + jnp.exp(-x))


def _params(sem):
    return pltpu.CompilerParams(dimension_semantics=sem, vmem_limit_bytes=VMEM_LIMIT)


def _full(shape):
    n = len(shape)
    return pl.BlockSpec(shape, lambda *_: (0,) * n)


def _proj_kernel(x_ref, ln_ref, wa_ref, wqkv_ref, wf_ref, wg_ref, fb_ref, qn_ref, kn_ref, gm_ref,
                 za_ref, q_ref, k_ref, v_ref, lf_ref, sg_ref, *, exact):
    x = x_ref[...]
    xn = x * lax.rsqrt(jnp.mean(x * x, axis=-1, keepdims=True) + RMS_EPS) * ln_ref[...]
    xm = xn if exact else xn.astype(BF16)
    za_ref[...] = _mm(xm, wa_ref[...], exact=exact)
    zqkv = _mm(xm, wqkv_ref[...], exact=exact)
    gm = gm_ref[...]

    def head_norm(z, w):
        ms = _mm(z * z, gm, exact=True)
        return z * lax.rsqrt(ms + RMS_EPS) * w

    q_ref[...] = (head_norm(zqkv[:, :B_WIDTH], qn_ref[...]) * (HEAD_DIM ** -0.5)).astype(q_ref.dtype)
    k_ref[...] = head_norm(zqkv[:, B_WIDTH:2 * B_WIDTH], kn_ref[...])
    v_ref[...] = zqkv[:, 2 * B_WIDTH:]
    zf = _mm(xm, wf_ref[...], exact=exact) + fb_ref[...]
    lf_ref[...] = jnp.minimum(zf, 0.0) - jnp.log(1.0 + jnp.exp(-jnp.abs(zf)))
    sg_ref[...] = _sigmoid(_mm(xm, wg_ref[...], exact=exact)).astype(sg_ref.dtype)


def _project(x2d, w, exact, tm):
    t = x2d.shape[0]
    wdt = F32 if exact else BF16
    act = F32 if exact else BF16
    row = lambda n: pl.BlockSpec((tm, n), lambda i: (i, 0))
    return pl.pallas_call(
        functools.partial(_proj_kernel, exact=exact),
        grid=(t // tm,),
        in_specs=[row(D_MODEL), _full((1, D_MODEL)), _full((D_MODEL, A_PROJ)), _full((D_MODEL, 3 * B_WIDTH)),
                  _full((D_MODEL, LANES)), _full((D_MODEL, 2 * D_MODEL)), _full((1, LANES)),
                  _full((1, B_WIDTH)), _full((1, B_WIDTH)), _full((B_WIDTH, B_WIDTH))],
        out_specs=[row(A_PROJ), row(B_WIDTH), row(B_WIDTH), row(B_WIDTH), row(LANES), row(2 * D_MODEL)],
        out_shape=[jax.ShapeDtypeStruct((t, A_PROJ), F32), jax.ShapeDtypeStruct((t, B_WIDTH), act),
                   jax.ShapeDtypeStruct((t, B_WIDTH), F32), jax.ShapeDtypeStruct((t, B_WIDTH), F32),
                   jax.ShapeDtypeStruct((t, LANES), F32), jax.ShapeDtypeStruct((t, 2 * D_MODEL), act)],
        compiler_params=_params(("arbitrary",)),
        name="proj_in",
    )(x2d, w['ln1'], w['wa'].astype(wdt), w['wqkv'].astype(wdt), w['wf'].astype(wdt), w['wg'].astype(wdt),
      w['fb'], w['qn'], w['kn'], w['gmean'])


def _cumsum_kernel(lf_ref, c_ref, ct_ref, *, nblk):
    r = lax.broadcasted_iota(jnp.int32, (ATT_BLOCK, ATT_BLOCK), 0)
    c = lax.broadcasted_iota(jnp.int32, (ATT_BLOCK, ATT_BLOCK), 1)
    tri = (c <= r).astype(F32)
    carry = jnp.zeros((1, LANES), F32)
    for i in range(nblk):
        x = lf_ref[0, i * ATT_BLOCK:(i + 1) * ATT_BLOCK, :]
        cs = _mm(tri, x, exact=True) + carry
        c_ref[0, i * ATT_BLOCK:(i + 1) * ATT_BLOCK, :] = cs
        ct_ref[0, i] = cs.T[:B_HEADS, :]
        carry = cs[ATT_BLOCK - 1:ATT_BLOCK, :]


def _cumsum(lf3):
    b, l, _ = lf3.shape
    nblk = l // ATT_BLOCK
    return pl.pallas_call(
        functools.partial(_cumsum_kernel, nblk=nblk),
        grid=(b,),
        in_specs=[pl.BlockSpec((1, l, LANES), lambda i: (i, 0, 0))],
        out_specs=[pl.BlockSpec((1, l, LANES), lambda i: (i, 0, 0)),
                   pl.BlockSpec((1, nblk, B_HEADS, ATT_BLOCK), lambda i: (i, 0, 0, 0))],
        out_shape=[jax.ShapeDtypeStruct((b, l, LANES), F32),
                   jax.ShapeDtypeStruct((b, nblk, B_HEADS, ATT_BLOCK), F32)],
        compiler_params=_params(("arbitrary",)),
        name="logf_cumsum",
    )(lf3)


def _attn_kernel(q_ref, k_ref, v_ref, c_ref, ct_ref, o_ref, kb_ref, vt_ref, ckb_ref, *, pad_front):
    qi = pl.program_id(1)
    first_blk = pad_front // ATT_BLOCK
    nkt = k_ref.shape[1] // KEY_TILE
    pair = lambda h: slice(LANES * (h // 2), LANES * (h // 2 + 1))

    @pl.when(qi < first_blk)
    def _():
        o_ref[...] = jnp.zeros(o_ref.shape, o_ref.dtype)

    @pl.when(qi == first_blk)
    def _():
        kb_ref[...] = k_ref[0].astype(BF16)
        for t in range(nkt):
            vt_ref[t] = v_ref[0, t * KEY_TILE:(t + 1) * KEY_TILE, :].T.astype(BF16)
        ri = lax.broadcasted_iota(jnp.int32, (LANES, LANES), 0)
        c = c_ref[0]
        for h in range(B_HEADS):
            ckb_ref[h] = _mm(c, (ri == h).astype(F32), exact=True)

    @pl.when(qi >= first_blk)
    def _():
        q = q_ref[0].astype(F32)
        lane = lax.broadcasted_iota(jnp.int32, (1, ATT_BLOCK), 1)
        qpos = qi * ATT_BLOCK + lane
        krow = lax.broadcasted_iota(jnp.int32, (KEY_TILE, 1), 0)
        qt = [jnp.where(lane // HEAD_DIM == h % 2, q[:, pair(h)], 0.0).T.astype(BF16) for h in range(B_HEADS)]
        cq = [ct_ref[0, qi, h:h + 1, :] for h in range(B_HEADS)]

        def tile(kj, carry, masked, live=None):
            kt = kb_ref[pl.ds(kj * KEY_TILE, KEY_TILE), :]
            if masked:
                kpos = kj * KEY_TILE + krow
                ok = (kpos >= pad_front) & (kpos <= qpos)
                if live is not None:
                    ok = ok & live
            out = []
            for h in range(B_HEADS):
                m, l, acc = carry[3 * h:3 * h + 3]
                s = jnp.dot(kt[:, pair(h)], qt[h], preferred_element_type=F32)
                s = s + cq[h] - ckb_ref[h, pl.ds(kj * KEY_TILE, KEY_TILE), :]
                if masked:
                    s = jnp.where(ok, s, NEG_INF)
                m_new = jnp.maximum(m, jnp.max(s, axis=0, keepdims=True))
                alpha = jnp.exp(m - m_new)
                p = jnp.exp(s - m_new)
                l = alpha * l + jnp.sum(p, axis=0, keepdims=True)
                pv = jnp.dot(vt_ref[kj, pair(h), :], p.astype(BF16), preferred_element_type=F32)
                out += [m_new, l, alpha * acc + pv]
            return tuple(out)

        init = (jnp.full((1, ATT_BLOCK), NEG_INF, F32), jnp.zeros((1, ATT_BLOCK), F32),
                jnp.zeros((LANES, ATT_BLOCK), F32)) * B_HEADS
        last = (qi * ATT_BLOCK) // KEY_TILE
        carry = tile(0, init, True)
        carry = lax.fori_loop(1, last, lambda kj, cr: tile(kj, cr, False), carry)
        carry = tile(last, carry, True, live=last > 0)
        outs = []
        for h in range(B_HEADS):
            l, acc = carry[3 * h + 1], carry[3 * h + 2]
            lo = HEAD_DIM * (h % 2)
            outs.append(acc[lo:lo + HEAD_DIM, :] / l)
        o_ref[0] = jnp.concatenate(outs, axis=0).T.astype(o_ref.dtype)


def _attention_prompt(q3, k3, v3, c3, ct4, pad_front):
    b, l, _ = q3.shape
    nq = l // ATT_BLOCK
    seq_blk = lambda n: pl.BlockSpec((1, l, n), lambda i, j: (i, 0, 0))
    return pl.pallas_call(
        functools.partial(_attn_kernel, pad_front=pad_front),
        grid=(b, nq),
        in_specs=[pl.BlockSpec((1, ATT_BLOCK, B_WIDTH), lambda i, j: (i, j, 0)),
                  seq_blk(B_WIDTH), seq_blk(B_WIDTH), seq_blk(LANES),
                  pl.BlockSpec((1, nq, B_HEADS, ATT_BLOCK), lambda i, j: (i, 0, 0, 0))],
        out_specs=pl.BlockSpec((1, ATT_BLOCK, B_WIDTH), lambda i, j: (i, j, 0)),
        out_shape=jax.ShapeDtypeStruct((b, l, B_WIDTH), BF16),
        scratch_shapes=[pltpu.VMEM((l, B_WIDTH), BF16), pltpu.VMEM((l // KEY_TILE, B_WIDTH, KEY_TILE), BF16),
                        pltpu.VMEM((B_HEADS, l, LANES), F32)],
        compiler_params=_params(("arbitrary", "arbitrary")),
        name="fox_prompt_attn",
    )(q3, k3, v3, c3, ct4)


def _attn_sample_kernel(pt_ref, q_ref, kn_ref, vn_ref, lfn_ref, *rest, n_steps, t_new, exact):
    pps = PAGES_PER_STEP
    kp_refs, vp_refs, lp_refs = rest[:pps], rest[pps:2 * pps], rest[2 * pps:3 * pps]
    o_ref = rest[3 * pps]
    m_ref, l_ref, acc_ref, car_ref, bq_ref = rest[3 * pps + 1:]
    j = pl.program_id(1)
    nrow = B_HEADS * t_new
    row_h = lax.broadcasted_iota(jnp.int32, (nrow, 1), 0) // t_new
    row_t = lax.broadcasted_iota(jnp.int32, (nrow, 1), 0) % t_new
    lane512 = lax.broadcasted_iota(jnp.int32, (1, B_WIDTH), 1)
    head_mask = (lane512 // HEAD_DIM == row_h).astype(F32)
    q = q_ref[0]
    qbd = jnp.broadcast_to(q[None], (B_HEADS, t_new, B_WIDTH)).reshape(nrow, B_WIDTH) * head_mask
    lane = lax.broadcasted_iota(jnp.int32, (1, LANES), 1)
    r_i = lax.broadcasted_iota(jnp.int32, (LANES, LANES), 0)
    c_i = lax.broadcasted_iota(jnp.int32, (LANES, LANES), 1)

    def to_rows(x8):
        n = x8.shape[-1]
        return jnp.broadcast_to(x8[:, None, :], (B_HEADS, t_new, n)).reshape(nrow, n)

    @pl.when(j == 0)
    def _():
        zrows = lambda u: jnp.concatenate([u, jnp.zeros((LANES - t_new, u.shape[1]), u.dtype)], axis=0)
        lfn = zrows(lfn_ref[0])
        cn = _mm((c_i <= r_i).astype(F32), lfn, exact=True)
        cn_t = cn.T[:B_HEADS, :]
        sel = (lane == row_h).astype(F32)
        cn_rows = jnp.broadcast_to(cn[None, :t_new], (B_HEADS, t_new, LANES)).reshape(nrow, LANES)
        cq = jnp.sum(cn_rows * sel, axis=-1, keepdims=True)
        bq_ref[...] = cq
        s = _mm(qbd, zrows(kn_ref[0]), NT, exact=exact) + cq - to_rows(cn_t)
        s = jnp.where(lane <= row_t, s, NEG_INF)
        m = jnp.max(s, axis=-1, keepdims=True)
        p = jnp.exp(s - m)
        m_ref[...] = m
        l_ref[...] = jnp.sum(p, axis=-1, keepdims=True)
        acc_ref[...] = _mm(p, zrows(vn_ref[0]), exact=exact)
        car_ref[...] = jnp.zeros(car_ref.shape, F32)

    cq = bq_ref[...]
    strict = (r_i > c_i).astype(F32)
    m, l, acc, car = m_ref[...], l_ref[...], acc_ref[...], car_ref[...]
    scores = []
    for i in range(pps):
        lf = lp_refs[i][0]
        lf_t = jnp.concatenate([lf, jnp.zeros((PAGE_SIZE, LANES - B_HEADS), F32)], axis=1).T[:B_HEADS, :]
        suf = _mm(lf_t, strict, exact=True) + car
        car = car + jnp.sum(lf_t, axis=-1, keepdims=True)
        scores.append(_mm(qbd, kp_refs[i][0], NT, exact=exact) + cq + to_rows(suf))
    m_new = m
    for s in scores:
        m_new = jnp.maximum(m_new, jnp.max(s, axis=-1, keepdims=True))
    alpha = jnp.exp(m - m_new)
    l = alpha * l
    acc = alpha * acc
    for i in range(pps):
        p = jnp.exp(scores[i] - m_new)
        l = l + jnp.sum(p, axis=-1, keepdims=True)
        acc = acc + _mm(p, vp_refs[i][0], exact=exact)
    m_ref[...], l_ref[...], acc_ref[...], car_ref[...] = m_new, l, acc, car

    @pl.when(j == n_steps - 1)
    def _():
        o = (acc / l) * head_mask
        o_ref[0] = jnp.sum(o.reshape(B_HEADS, t_new, B_WIDTH), axis=0)


def _attention_sample(q3, k3, v3, lf3, cache_k, cache_v, cache_lf, page_table, exact):
    b, t_new, _ = q3.shape
    n_pages = page_table.shape[1]
    pps = PAGES_PER_STEP
    n_steps = n_pages // pps
    nrow = B_HEADS * t_new

    def page_map(i):
        return lambda bi, j, pt: (pt[bi, n_pages - 1 - (j * pps + i)], 0, 0)

    new = lambda n: pl.BlockSpec((1, t_new, n), lambda bi, j, pt: (bi, 0, 0))
    in_specs = [new(B_WIDTH), new(B_WIDTH), new(B_WIDTH), new(LANES)]
    in_specs += [pl.BlockSpec((1, PAGE_SIZE, B_WIDTH), page_map(i)) for i in range(pps)]
    in_specs += [pl.BlockSpec((1, PAGE_SIZE, B_WIDTH), page_map(i)) for i in range(pps)]
    in_specs += [pl.BlockSpec((1, PAGE_SIZE, B_HEADS), page_map(i)) for i in range(pps)]
    grid_spec = pltpu.PrefetchScalarGridSpec(
        num_scalar_prefetch=1, grid=(b, n_steps), in_specs=in_specs,
        out_specs=pl.BlockSpec((1, t_new, B_WIDTH), lambda bi, j, pt: (bi, 0, 0)),
        scratch_shapes=[pltpu.VMEM((nrow, 1), F32), pltpu.VMEM((nrow, 1), F32), pltpu.VMEM((nrow, B_WIDTH), F32),
                        pltpu.VMEM((B_HEADS, 1), F32), pltpu.VMEM((nrow, 1), F32)])
    return pl.pallas_call(
        functools.partial(_attn_sample_kernel, n_steps=n_steps, t_new=t_new, exact=exact),
        grid_spec=grid_spec,
        out_shape=jax.ShapeDtypeStruct((b, t_new, B_WIDTH), F32),
        compiler_params=_params(("arbitrary", "arbitrary")),
        name="fox_sample_attn",
    )(page_table, q3, k3, v3, lf3, *([cache_k] * pps), *([cache_v] * pps), *([cache_lf] * pps))


def _rwkv_kernel(za_ref, s0_ref, sh0_ref, mu_ref, w0_ref, a0_ref, kk_ref, ka_ref, rk_ref, lw_ref, lb_ref,
                 w2_ref, a2_ref, g2_ref, gs_ref, oa_ref, sout_ref, prev_ref, st_ref, *, valid_rows, exact):
    c = za_ref.shape[1]
    j = pl.program_id(1)

    @pl.when(j == 0)
    def _():
        prev_ref[...] = sh0_ref[0]
        st_ref[...] = s0_ref[0]

    za = za_ref[0]
    row = lax.broadcasted_iota(jnp.int32, (c, 1), 0)
    zprev = jnp.where(row == 0, prev_ref[...], pltpu.roll(za, 1, 0))
    prev_ref[...] = za[c - 1:c, :]
    zs = za + (zprev - za) * mu_ref[...]
    i1, i2, i3 = A_WIDTH, 2 * A_WIDTH, 3 * A_WIDTH
    i4 = i3 + W_LORA
    i5 = i4 + A_LORA
    r, k, v = zs[:, :i1], zs[:, i1:i2], zs[:, i2:i3]
    wl, al, gl = zs[:, i3:i4], zs[:, i4:i5], zs[:, i5:]
    logw = -math.exp(-0.5) * _sigmoid(w0_ref[...] + _mm(jnp.tanh(wl), w2_ref[...], exact=exact))
    a = _sigmoid(a0_ref[...] + _mm(al, a2_ref[...], exact=exact))
    g = _mm(_sigmoid(gl), g2_ref[...], exact=exact)
    kk = k * kk_ref[...]
    kk = kk * lax.rsqrt(jnp.maximum(_mm(kk * kk, gs_ref[...], exact=True), 1e-24))
    kmod = k * (1.0 + (a - 1.0) * ka_ref[...])
    bb = kk * a
    if valid_rows < c:
        live = row < valid_rows
        logw = jnp.where(live, logw, 0.0)
        v, kmod, bb, kk = (jnp.where(live, u, 0.0) for u in (v, kmod, bb, kk))

    ri = lax.broadcasted_iota(jnp.int32, (c, c), 0)
    ci = lax.broadcasted_iota(jnp.int32, (c, c), 1)
    incl = (ci <= ri).astype(F32)
    strict = (ci < ri).astype(F32)
    eye = (ci == ri).astype(F32)
    gcum = _mm(incl, logw, exact=True)
    glast = gcum[c - 1:c, :]
    p_in, p_out = jnp.exp(gcum), jnp.exp(-gcum)
    tail = jnp.exp(glast - gcum)
    kt = kk * jnp.exp(gcum - logw)
    bt = bb * p_out
    kkt = kmod * p_out
    rt = r * p_in
    bh = bb * tail
    kh = kmod * tail
    pc = jnp.exp(glast)

    heads = range(A_HEADS)
    sls = [slice(h * HEAD_DIM, (h + 1) * HEAD_DIM) for h in heads]
    mm = functools.partial(_mm, exact=exact)
    m0 = [st_ref[h] for h in heads]
    vh = [v[:, sl] for sl in sls]
    lhs = [jnp.concatenate([kt[:, sl], rt[:, sl]], axis=0) for sl in sls]
    ab = [mm(lhs[h], bt[:, sls[h]], NT) for h in heads]
    ak = [mm(lhs[h], kkt[:, sls[h]], NT) for h in heads]
    lm = [mm(lhs[h], m0[h], NT) for h in heads]
    pw = [-(ab[h][:c] * strict) for h in heads]
    x = [eye + pw[h] for h in heads]
    n = 1
    while 2 * n < c:
        pw = [mm(pw[h], pw[h]) for h in heads]
        x = [x[h] + mm(x[h], pw[h]) for h in heads]
        n *= 2
    base = [lm[h][:c] + mm(ak[h][:c] * strict, vh[h]) for h in heads]
    u = [mm(x[h], base[h]) for h in heads]
    o = [lm[h][c:] - mm(ab[h][c:] * incl, u[h]) + mm(ak[h][c:] * incl, vh[h]) for h in heads]
    for h in heads:
        st_ref[h] = m0[h] * pc[:, sls[h]] + mm(vh[h], kh[:, sls[h]], TN) - mm(u[h], bh[:, sls[h]], TN)
    outs = []
    for h in heads:
        sl = sls[h]
        mean = jnp.mean(o[h], axis=-1, keepdims=True)
        var = jnp.mean(jnp.square(o[h] - mean), axis=-1, keepdims=True)
        on = (o[h] - mean) * lax.rsqrt(var + GN_EPS) * lw_ref[:, sl] + lb_ref[:, sl]
        bonus = jnp.sum(r[:, sl] * kmod[:, sl] * rk_ref[:, sl], axis=-1, keepdims=True) * vh[h]
        outs.append((on + bonus) * g[:, sl])
    oa_ref[0] = jnp.concatenate(outs, axis=-1).astype(oa_ref.dtype)

    @pl.when(j == pl.num_programs(1) - 1)
    def _():
        sout_ref[0] = st_ref[...]


def _rwkv(za3, s0, shift0, w, valid_rows, exact):
    b, l, _ = za3.shape
    c = CHUNK
    vec = lambda n: _full((1, n))
    wdt = F32 if exact else BF16
    return pl.pallas_call(
        functools.partial(_rwkv_kernel, valid_rows=valid_rows, exact=exact),
        grid=(b, l // c),
        in_specs=[pl.BlockSpec((1, c, A_PROJ), lambda i, j: (i, j, 0)),
                  pl.BlockSpec((1, A_HEADS, HEAD_DIM, HEAD_DIM), lambda i, j: (i, 0, 0, 0)),
                  pl.BlockSpec((1, 1, A_PROJ), lambda i, j: (i, 0, 0)),
                  vec(A_PROJ), vec(A_WIDTH), vec(A_WIDTH), vec(A_WIDTH), vec(A_WIDTH), vec(A_WIDTH),
                  vec(A_WIDTH), vec(A_WIDTH),
                  _full((W_LORA, A_WIDTH)), _full((A_LORA, A_WIDTH)), _full((G_LORA, A_WIDTH)),
                  _full((A_WIDTH, A_WIDTH))],
        out_specs=[pl.BlockSpec((1, c, A_WIDTH), lambda i, j: (i, j, 0)),
                   pl.BlockSpec((1, A_HEADS, HEAD_DIM, HEAD_DIM), lambda i, j: (i, 0, 0, 0))],
        out_shape=[jax.ShapeDtypeStruct((b, l, A_WIDTH), F32 if exact else BF16),
                   jax.ShapeDtypeStruct((b, A_HEADS, HEAD_DIM, HEAD_DIM), F32)],
        scratch_shapes=[pltpu.VMEM((1, A_PROJ), F32), pltpu.VMEM((A_HEADS, HEAD_DIM, HEAD_DIM), F32)],
        compiler_params=_params(("arbitrary", "arbitrary")),
        name="rwkv7_chunked",
    )(za3, s0, shift0[:, None, :], w['mu'], w['w0'], w['a0'], w['k_k'], w['k_a'], w['r_k'], w['lnx_w'], w['lnx_b'],
      w['w2'].astype(wdt), w['a2'].astype(wdt), w['g2'].astype(wdt), w['gsum'])


def _merge_kernel(oa_ref, ob_ref, sg_ref, x_ref, pa_ref, pb_ref, wo_ref, ln2_ref, wr_ref,
                  h_ref, xn_ref, lg_ref, *, exact):
    sg = sg_ref[0].astype(F32)
    ya = _mm(oa_ref[0], pa_ref[...], exact=exact)
    yb = _mm(ob_ref[0], pb_ref[...], exact=exact)
    merged = sg[:, :D_MODEL] * ya + sg[:, D_MODEL:] * yb
    h = x_ref[0] + _mm(merged, wo_ref[...], exact=exact)
    h_ref[0] = h
    xn = h * lax.rsqrt(jnp.mean(h * h, axis=-1, keepdims=True) + RMS_EPS) * ln2_ref[...]
    xn_ref[0] = xn.astype(xn_ref.dtype)
    lg_ref[0] = _mm(xn, wr_ref[...], exact=exact)


def _merge(oa3, ob3, sg3, x3, w, row_off, exact, tm):
    b, s, _ = x3.shape
    wdt = F32 if exact else BF16
    blk = lambda n, off: pl.BlockSpec((1, tm, n), lambda i, j: (i, j + off, 0))
    return pl.pallas_call(
        functools.partial(_merge_kernel, exact=exact),
        grid=(b, s // tm),
        in_specs=[blk(A_WIDTH, row_off), blk(B_WIDTH, row_off), blk(2 * D_MODEL, row_off), blk(D_MODEL, 0),
                  _full((A_WIDTH, D_MODEL)), _full((B_WIDTH, D_MODEL)), _full((D_MODEL, D_MODEL)),
                  _full((1, D_MODEL)), _full((D_MODEL, LANES))],
        out_specs=[blk(D_MODEL, 0), blk(D_MODEL, 0), blk(LANES, 0)],
        out_shape=[jax.ShapeDtypeStruct((b, s, D_MODEL), F32),
                   jax.ShapeDtypeStruct((b, s, D_MODEL), F32 if exact else BF16),
                   jax.ShapeDtypeStruct((b, s, LANES), F32)],
        compiler_params=_params(("arbitrary", "arbitrary")),
        name="merge_out",
    )(oa3, ob3, sg3, x3, w['proj_a'].astype(wdt), w['proj_b'].astype(wdt), w['w_out'].astype(wdt),
      w['ln2'], w['wr'].astype(wdt))


def _route(lg):
    lane = lax.broadcasted_iota(jnp.int32, lg.shape, 1)
    big = jnp.int32(1 << 20)
    is_g = (lane >= N_EXPERTS) & (lane < N_EXPERTS + N_GROUPS)
    gl = jnp.where(is_g, lg, NEG_INF)
    gmax = jnp.max(gl, axis=-1, keepdims=True)
    gp = 1.0 / jnp.sum(jnp.where(is_g, jnp.exp(gl - gmax), 0.0), axis=-1, keepdims=True)
    gi = jnp.min(jnp.where(is_g & (gl == gmax), lane, big), axis=-1, keepdims=True) - N_EXPERTS
    in_g = (lane < N_EXPERTS) & (lane // EXP_PER_GROUP == gi)
    sel = jnp.where(in_g, lg, NEG_INF)
    v1 = jnp.max(sel, axis=-1, keepdims=True)
    i1 = jnp.min(jnp.where(in_g & (sel == v1), lane, big), axis=-1, keepdims=True)
    sel2 = jnp.where(lane == i1, NEG_INF, sel)
    v2 = jnp.max(sel2, axis=-1, keepdims=True)
    i2 = jnp.min(jnp.where(in_g & (lane != i1) & (sel2 == v2), lane, big), axis=-1, keepdims=True)
    e2 = jnp.exp(v2 - v1)
    w1 = gp / (1.0 + e2)
    w2 = gp * e2 / (1.0 + e2)
    return jnp.where(lane == i1, w1, 0.0) + jnp.where(lane == i2, w2, 0.0)


def _moe_kernel(xn_ref, lg_ref, h_ref, w1_ref, w3_ref, w2_ref, lnf_ref, y_ref, acc_ref, cmb_ref, *, exact):
    e = pl.program_id(1)

    @pl.when(e == 0)
    def _():
        acc_ref[...] = jnp.zeros(acc_ref.shape, F32)
        cmb_ref[...] = _route(lg_ref[...])

    x = xn_ref[...]
    a = _mm(x, w1_ref[0], exact=exact)
    hid = a * _sigmoid(a) * _mm(x, w3_ref[0], exact=exact)
    lane = lax.broadcasted_iota(jnp.int32, (1, LANES), 1)
    ce = jnp.sum(jnp.where(lane == e, cmb_ref[...], 0.0), axis=-1, keepdims=True)
    acc_ref[...] += _mm(hid * ce, w2_ref[0], exact=exact)

    @pl.when(e == N_EXPERTS - 1)
    def _():
        y = h_ref[...] + acc_ref[...]
        y_ref[...] = y * lax.rsqrt(jnp.mean(y * y, axis=-1, keepdims=True) + RMS_EPS) * lnf_ref[...]


def _moe(xn2d, lg2d, h2d, w, exact, tm):
    t = xn2d.shape[0]
    wdt = F32 if exact else BF16
    row = lambda n: pl.BlockSpec((tm, n), lambda i, e: (i, 0))
    return pl.pallas_call(
        functools.partial(_moe_kernel, exact=exact),
        grid=(t // tm, N_EXPERTS),
        in_specs=[row(D_MODEL), row(LANES), row(D_MODEL),
                  pl.BlockSpec((1, D_MODEL, EXPERT_FF), lambda i, e: (e, 0, 0)),
                  pl.BlockSpec((1, D_MODEL, EXPERT_FF), lambda i, e: (e, 0, 0)),
                  pl.BlockSpec((1, EXPERT_FF, D_MODEL), lambda i, e: (e, 0, 0)),
                  pl.BlockSpec((1, D_MODEL), lambda i, e: (0, 0))],
        out_specs=row(D_MODEL),
        out_shape=jax.ShapeDtypeStruct((t, D_MODEL), F32),
        scratch_shapes=[pltpu.VMEM((tm, D_MODEL), F32), pltpu.VMEM((tm, LANES), F32)],
        compiler_params=_params(("arbitrary", "arbitrary")),
        name="hier_moe",
    )(xn2d, lg2d, h2d, w['w1'].astype(wdt), w['w3'].astype(wdt), w['w2'].astype(wdt), w['lnf'])


def _layer_weights(l, ln1_w, w_in, rwkv_mu, rwkv_w0, rwkv_w2, rwkv_a0, rwkv_a2, rwkv_g2, rwkv_k_k, rwkv_k_a,
                   rwkv_r_k, rwkv_lnx_w, rwkv_lnx_b, fox_q_norm, fox_k_norm, fox_f_bias, proj_a, proj_b, w_out,
                   ln2_w, router_grp, router_exp, exp_w1, exp_w3, exp_w2, ln_f):
    wi = w_in[l]
    c1 = A_PROJ
    c2 = c1 + 3 * B_WIDTH
    c3 = c2 + B_HEADS
    head_of = jnp.arange(B_WIDTH) // HEAD_DIM
    same_head = (head_of[:, None] == head_of[None, :]).astype(F32)
    row = lambda u: u.reshape(1, -1).astype(F32)
    return dict(
        ln1=row(ln1_w[l]), wa=wi[:, :c1], wqkv=wi[:, c1:c2],
        wf=jnp.pad(wi[:, c2:c3], ((0, 0), (0, LANES - B_HEADS))), wg=wi[:, c3:],
        fb=jnp.pad(row(fox_f_bias[l]), ((0, 0), (0, LANES - B_HEADS))),
        qn=row(jnp.tile(fox_q_norm[l], B_HEADS)), kn=row(jnp.tile(fox_k_norm[l], B_HEADS)),
        gmean=same_head / HEAD_DIM, gsum=same_head,
        mu=row(rwkv_mu[l]), w0=row(rwkv_w0[l]), a0=row(rwkv_a0[l]), k_k=row(rwkv_k_k[l]), k_a=row(rwkv_k_a[l]),
        r_k=row(rwkv_r_k[l]), lnx_w=row(rwkv_lnx_w[l]), lnx_b=row(rwkv_lnx_b[l]),
        w2=rwkv_w2[l], a2=rwkv_a2[l], g2=rwkv_g2[l],
        proj_a=proj_a[l], proj_b=proj_b[l], w_out=w_out[l], ln2=row(ln2_w[l]),
        wr=jnp.pad(jnp.concatenate([router_exp[l], router_grp[l]], axis=1),
                   ((0, 0), (0, LANES - N_EXPERTS - N_GROUPS))),
        w1=exp_w1[l], w3=exp_w3[l], w2e=exp_w2[l], lnf=row(ln_f))


def kernel(x_prompt, x_sample, cache_k, cache_v, cache_logf, state_rwkv, state_shift, page_table,
           meta_tokens, ln1_w, w_in, rwkv_mu, rwkv_w0, rwkv_w2, rwkv_a0, rwkv_a2, rwkv_g2,
           rwkv_k_k, rwkv_k_a, rwkv_r_k, rwkv_lnx_w, rwkv_lnx_b, fox_q_norm, fox_k_norm, fox_f_bias,
           proj_a, proj_b, w_out, ln2_w, router_grp, router_exp, exp_w1, exp_w3, exp_w2, ln_f):
    depth = w_in.shape[0]
    assert depth == 1, "single trunk layer"
    b_p, seq, _ = x_prompt.shape
    b_s, t_new, _ = x_sample.shape
    assert seq % ROW_TILE == 0 and t_new <= CHUNK and page_table.shape[1] % PAGES_PER_STEP == 0
    w = _layer_weights(0, ln1_w, w_in, rwkv_mu, rwkv_w0, rwkv_w2, rwkv_a0, rwkv_a2, rwkv_g2, rwkv_k_k, rwkv_k_a,
                       rwkv_r_k, rwkv_lnx_w, rwkv_lnx_b, fox_q_norm, fox_k_norm, fox_f_bias, proj_a, proj_b,
                       w_out, ln2_w, router_grp, router_exp, exp_w1, exp_w3, exp_w2, ln_f)
    wm = dict(w, w2=w['w2e'])

    pad_front = ROW_TILE - N_META
    l_pad = ROW_TILE + seq
    l_real = N_META + seq
    head = jnp.concatenate([jnp.zeros((pad_front, D_MODEL), F32), meta_tokens.astype(F32)], axis=0)
    hp = jnp.concatenate([jnp.broadcast_to(head[None], (b_p, ROW_TILE, D_MODEL)), x_prompt], axis=1)
    za, q, k, v, lf, sg = _project(hp.reshape(b_p * l_pad, D_MODEL), w, False, ROW_TILE)
    r3 = lambda u: u.reshape(b_p, l_pad, u.shape[-1])
    c3, ct4 = _cumsum(r3(lf))
    ob = _attention_prompt(r3(q), r3(k), r3(v), c3, ct4, pad_front)
    oa, s_p = _rwkv(r3(za), jnp.zeros((b_p, A_HEADS, HEAD_DIM, HEAD_DIM), F32), jnp.zeros((b_p, A_PROJ), F32),
                    w, CHUNK, False)
    h_p, xn_p, lg_p = _merge(oa, ob, r3(sg), x_prompt, w, 1, False, ROW_TILE)
    flat = lambda u: u.reshape(b_p * seq, u.shape[-1])
    y_prompt = _moe(flat(xn_p), flat(lg_p), flat(h_p), wm, False, math.gcd(b_p * seq, MOE_TILE))
    y_prompt = y_prompt.reshape(b_p, seq, D_MODEL)
    k_p = r3(k)[:, pad_front:].reshape(1, b_p, l_real, B_HEADS, HEAD_DIM)
    v_p = r3(v)[:, pad_front:].reshape(1, b_p, l_real, B_HEADS, HEAD_DIM)
    lf_p = r3(lf)[:, pad_front:, :B_HEADS][None]
    sh_p = r3(za)[:, l_pad - 1][None]

    n_s = b_s * t_new
    za_s, q_s, k_s, v_s, lf_s, sg_s = _project(x_sample.reshape(n_s, D_MODEL), w, True, n_s)
    s3 = lambda u: u.reshape(b_s, t_new, u.shape[-1])
    ob_s = _attention_sample(s3(q_s), s3(k_s), s3(v_s), s3(lf_s),
                             cache_k[0].reshape(-1, PAGE_SIZE, B_WIDTH), cache_v[0].reshape(-1, PAGE_SIZE, B_WIDTH),
                             cache_logf[0], page_table, False)
    za_pad = jnp.pad(s3(za_s), ((0, 0), (0, CHUNK - t_new), (0, 0)))
    oa_s, s_s = _rwkv(za_pad, state_rwkv[0], state_shift[0], w, t_new, True)
    one = lambda u: u.reshape(1, n_s, u.shape[-1])
    h_s, xn_s, lg_s = _merge(one(oa_s[:, :t_new]), one(ob_s), one(sg_s), one(x_sample), w, 0, True, n_s)
    y_sample = _moe(xn_s[0], lg_s[0], h_s[0], wm, True, n_s).reshape(b_s, t_new, D_MODEL)
    k_sn = s3(k_s).reshape(1, b_s, t_new, B_HEADS, HEAD_DIM)
    v_sn = s3(v_s).reshape(1, b_s, t_new, B_HEADS, HEAD_DIM)
    lf_sn = s3(lf_s)[:, :, :B_HEADS][None]
    sh_s = s3(za_s)[:, t_new - 1][None]
    return (y_prompt, y_sample, k_p, v_p, lf_p, s_p[None], sh_p, k_sn, v_sn, lf_sn, s_s[None], sh_s)
```

```python
import functools
import math

import jax
import jax.numpy as jnp
from jax import lax
from jax.experimental import pallas as pl
from jax.experimental.pallas import tpu as pltpu

D_MODEL = 1024
PAGE_SIZE = 128
N_META = 16
HEAD_DIM = 64
A_WIDTH = D_MODEL // 2
A_HEADS = A_WIDTH // HEAD_DIM
B_WIDTH = D_MODEL // 2
B_HEADS = B_WIDTH // HEAD_DIM
W_LORA = 64
A_LORA = 64
G_LORA = 128
A_PROJ = 3 * A_WIDTH + W_LORA + A_LORA + G_LORA
B_PROJ = 3 * B_WIDTH + B_HEADS
N_GROUPS = 4
EXP_PER_GROUP = 8
N_EXPERTS = N_GROUPS * EXP_PER_GROUP
EXPERT_FF = D_MODEL // 4
RMS_EPS = 1e-6
GN_EPS = 64e-5
NEG_INF = -1e30

LANES = 128
ROW_TILE = 256
ATT_BLOCK = 128
KEY_TILE = 256
CHUNK = 64
SEQS_PER_STEP = 4
PAGES_PER_STEP = 32
MOE_TILE = 1024
ROUTE_TILE = 512
VMEM_LIMIT = 56 * 1024 * 1024

F32 = jnp.float32
BF16 = jnp.bfloat16
NN = (((1,), (0,)), ((), ()))
NT = (((1,), (1,)), ((), ()))
TN = (((0,), (0,)), ((), ()))


def _mm(a, b, dims=NN, exact=False):
    if exact:
        return lax.dot_general(a.astype(F32), b.astype(F32), dims,
                               precision=lax.Precision.HIGHEST, preferred_element_type=F32)
    return lax.dot_general(a.astype(BF16), b.astype(BF16), dims, preferred_element_type=F32)


def _mm_x(a, b, dims=NN, data="lhs"):
    x, m = (a, b) if data == "lhs" else (b, a)
    hi = x.astype(BF16)
    rest = x - hi.astype(F32)
    mid = rest.astype(BF16)
    lo = (rest - mid.astype(F32)).astype(BF16)
    mb = m.astype(BF16)
    if data == "lhs":
        dot = lambda u: lax.dot_general(u, mb, dims, preferred_element_type=F32)
    else:
        dot = lambda u: lax.dot_general(mb, u, dims, preferred_element_type=F32)
    return dot(hi) + dot(mid) + dot(lo)


def _sigmoid(x):
    return 1.0 / (1.0 + jnp.exp(-x))


def _params(sem):
    return pltpu.CompilerParams(dimension_semantics=sem, vmem_limit_bytes=VMEM_LIMIT)


def _full(shape):
    n = len(shape)
    return pl.BlockSpec(shape, lambda *_: (0,) * n)


def _proj_body(x, ln_ref, wa_ref, wqkv_ref, wf_ref, wg_ref, fb_ref, qn_ref, kn_ref, gm_ref,
               za_ref, q_ref, k_ref, v_ref, lf_ref, sg_ref, *, exact):
    xn = x * lax.rsqrt(jnp.mean(x * x, axis=-1, keepdims=True) + RMS_EPS) * ln_ref[...]
    xm = xn if exact else xn.astype(BF16)
    za_ref[...] = _mm(xm, wa_ref[...], exact=exact)
    zqkv = _mm(xm, wqkv_ref[...], exact=exact)
    gm = gm_ref[...]

    def head_norm(z, w):
        ms = _mm(z * z, gm, exact=exact)
        return z * lax.rsqrt(ms + RMS_EPS) * w

    q_ref[...] = (head_norm(zqkv[:, :B_WIDTH], qn_ref[...]) * (HEAD_DIM ** -0.5)).astype(q_ref.dtype)
    k_ref[...] = head_norm(zqkv[:, B_WIDTH:2 * B_WIDTH], kn_ref[...])
    v_ref[...] = zqkv[:, 2 * B_WIDTH:]
    zf = _mm(xm, wf_ref[...], exact=exact) + fb_ref[...]
    lf_ref[...] = jnp.minimum(zf, 0.0) - jnp.log(1.0 + jnp.exp(-jnp.abs(zf)))
    sg_ref[...] = _sigmoid(_mm(xm, wg_ref[...], exact=exact)).astype(sg_ref.dtype)


def _proj_kernel(x_ref, *refs, exact):
    _proj_body(x_ref[...], *refs, exact=exact)


def _proj_prompt_kernel(head_ref, x_ref, *refs, exact):
    x = jnp.where(pl.program_id(1) == 0, head_ref[...], x_ref[0])
    _proj_body(x, *refs, exact=exact)


def _project(x, w, exact, tm, head=None):
    wdt = F32 if exact else BF16
    act = F32 if exact else BF16
    if head is None:
        t = x.shape[0]
        grid = (t // tm,)
        row = lambda n: pl.BlockSpec((tm, n), lambda i: (i, 0))
        x_specs, x_args, body, sem = [row(D_MODEL)], (x,), _proj_kernel, ("arbitrary",)
    else:
        b, seq, _ = x.shape
        nt = 1 + seq // tm
        t = b * nt * tm
        grid = (b, nt)
        row = lambda n: pl.BlockSpec((tm, n), lambda i, j: (i * nt + j, 0))
        x_specs = [pl.BlockSpec((tm, D_MODEL), lambda i, j: (0, 0)),
                   pl.BlockSpec((1, tm, D_MODEL), lambda i, j: (i, jnp.maximum(j - 1, 0), 0))]
        x_args, body, sem = (head, x), _proj_prompt_kernel, ("arbitrary", "arbitrary")
    full = lambda shape: pl.BlockSpec(shape, lambda *_: (0,) * len(shape))
    return pl.pallas_call(
        functools.partial(body, exact=exact),
        grid=grid,
        in_specs=x_specs + [full((1, D_MODEL)), full((D_MODEL, A_PROJ)), full((D_MODEL, 3 * B_WIDTH)),
                            full((D_MODEL, LANES)), full((D_MODEL, 2 * D_MODEL)), full((1, LANES)),
                            full((1, B_WIDTH)), full((1, B_WIDTH)), full((B_WIDTH, B_WIDTH))],
        out_specs=[row(A_PROJ), row(B_WIDTH), row(B_WIDTH), row(B_WIDTH), row(LANES), row(2 * D_MODEL)],
        out_shape=[jax.ShapeDtypeStruct((t, A_PROJ), F32), jax.ShapeDtypeStruct((t, B_WIDTH), act),
                   jax.ShapeDtypeStruct((t, B_WIDTH), F32), jax.ShapeDtypeStruct((t, B_WIDTH), F32),
                   jax.ShapeDtypeStruct((t, LANES), F32), jax.ShapeDtypeStruct((t, 2 * D_MODEL), act)],
        compiler_params=_params(sem),
        name="proj_in",
    )(*x_args, w['ln1'], w['wa'].astype(wdt), w['wqkv'].astype(wdt), w['wf'].astype(wdt), w['wg'].astype(wdt),
      w['fb'], w['qn'], w['kn'], w['gmean'])


def _cumsum_kernel(lf_ref, c_ref, ct_ref, *, nblk):
    r = lax.broadcasted_iota(jnp.int32, (ATT_BLOCK, ATT_BLOCK), 0)
    c = lax.broadcasted_iota(jnp.int32, (ATT_BLOCK, ATT_BLOCK), 1)
    tri = (c <= r).astype(F32)
    carry = jnp.zeros((1, LANES), F32)
    for i in range(nblk):
        x = lf_ref[0, i * ATT_BLOCK:(i + 1) * ATT_BLOCK, :]
        cs = _mm_x(tri, x, data="rhs") + carry
        c_ref[0, i * ATT_BLOCK:(i + 1) * ATT_BLOCK, :] = cs
        ct_ref[0, i] = cs.T[:B_HEADS, :]
        carry = cs[ATT_BLOCK - 1:ATT_BLOCK, :]


def _cumsum(lf3):
    b, l, _ = lf3.shape
    nblk = l // ATT_BLOCK
    return pl.pallas_call(
        functools.partial(_cumsum_kernel, nblk=nblk),
        grid=(b,),
        in_specs=[pl.BlockSpec((1, l, LANES), lambda i: (i, 0, 0))],
        out_specs=[pl.BlockSpec((1, l, LANES), lambda i: (i, 0, 0)),
                   pl.BlockSpec((1, nblk, B_HEADS, ATT_BLOCK), lambda i: (i, 0, 0, 0))],
        out_shape=[jax.ShapeDtypeStruct((b, l, LANES), F32),
                   jax.ShapeDtypeStruct((b, nblk, B_HEADS, ATT_BLOCK), F32)],
        compiler_params=_params(("arbitrary",)),
        name="logf_cumsum",
    )(lf3)


def _attn_kernel(q_ref, k_ref, v_ref, c_ref, ct_ref, o_ref, kb_ref, vt_ref, ckb_ref, *, pad_front):
    qi = pl.program_id(1)
    first_blk = pad_front // ATT_BLOCK
    nkt = k_ref.shape[1] // KEY_TILE
    pair = lambda h: slice(LANES * (h // 2), LANES * (h // 2 + 1))

    @pl.when(qi < first_blk)
    def _():
        o_ref[...] = jnp.zeros(o_ref.shape, o_ref.dtype)

    @pl.when(qi == first_blk)
    def _():
        kb_ref[...] = k_ref[0].astype(BF16)
        for t in range(nkt):
            vt_ref[t] = v_ref[0, t * KEY_TILE:(t + 1) * KEY_TILE, :].T.astype(BF16)
        ri = lax.broadcasted_iota(jnp.int32, (LANES, LANES), 0)
        c = c_ref[0]
        for h in range(B_HEADS):
            ckb_ref[h] = _mm_x(c, (ri == h).astype(F32))

    @pl.when(qi >= first_blk)
    def _():
        q = q_ref[0].astype(F32)
        lane = lax.broadcasted_iota(jnp.int32, (1, ATT_BLOCK), 1)
        qpos = qi * ATT_BLOCK + lane
        krow = lax.broadcasted_iota(jnp.int32, (KEY_TILE, 1), 0)
        qt = [jnp.where(lane // HEAD_DIM == h % 2, q[:, pair(h)], 0.0).T.astype(BF16) for h in range(B_HEADS)]
        cq = [ct_ref[0, qi, h:h + 1, :] for h in range(B_HEADS)]

        def tile(kj, carry, masked, live=None):
            kt = kb_ref[pl.ds(kj * KEY_TILE, KEY_TILE), :]
            if masked:
                kpos = kj * KEY_TILE + krow
                ok = (kpos >= pad_front) & (kpos <= qpos)
                if live is not None:
                    ok = ok & live
            out = []
            for h in range(B_HEADS):
                m, l, acc = carry[3 * h:3 * h + 3]
                s = jnp.dot(kt[:, pair(h)], qt[h], preferred_element_type=F32)
                s = s + cq[h] - ckb_ref[h, pl.ds(kj * KEY_TILE, KEY_TILE), :]
                if masked:
                    s = jnp.where(ok, s, NEG_INF)
                m_new = jnp.maximum(m, jnp.max(s, axis=0, keepdims=True))
                alpha = jnp.exp(m - m_new)
                p = jnp.exp(s - m_new)
                l = alpha * l + jnp.sum(p, axis=0, keepdims=True)
                pv = jnp.dot(vt_ref[kj, pair(h), :], p.astype(BF16), preferred_element_type=F32)
                out += [m_new, l, alpha * acc + pv]
            return tuple(out)

        init = (jnp.full((1, ATT_BLOCK), NEG_INF, F32), jnp.zeros((1, ATT_BLOCK), F32),
                jnp.zeros((LANES, ATT_BLOCK), F32)) * B_HEADS
        last = (qi * ATT_BLOCK) // KEY_TILE
        carry = tile(0, init, True)
        carry = lax.fori_loop(1, last, lambda kj, cr: tile(kj, cr, False), carry)
        carry = tile(last, carry, True, live=last > 0)
        outs = []
        for h in range(B_HEADS):
            l, acc = carry[3 * h + 1], carry[3 * h + 2]
            lo = HEAD_DIM * (h % 2)
            outs.append(acc[lo:lo + HEAD_DIM, :] / l)
        o_ref[0] = jnp.concatenate(outs, axis=0).T.astype(o_ref.dtype)


def _attention_prompt(q3, k3, v3, c3, ct4, pad_front):
    b, l, _ = q3.shape
    nq = l // ATT_BLOCK
    seq_blk = lambda n: pl.BlockSpec((1, l, n), lambda i, j: (i, 0, 0))
    return pl.pallas_call(
        functools.partial(_attn_kernel, pad_front=pad_front),
        grid=(b, nq),
        in_specs=[pl.BlockSpec((1, ATT_BLOCK, B_WIDTH), lambda i, j: (i, j, 0)),
                  seq_blk(B_WIDTH), seq_blk(B_WIDTH), seq_blk(LANES),
                  pl.BlockSpec((1, nq, B_HEADS, ATT_BLOCK), lambda i, j: (i, 0, 0, 0))],
        out_specs=pl.BlockSpec((1, ATT_BLOCK, B_WIDTH), lambda i, j: (i, j, 0)),
        out_shape=jax.ShapeDtypeStruct((b, l, B_WIDTH), BF16),
        scratch_shapes=[pltpu.VMEM((l, B_WIDTH), BF16), pltpu.VMEM((l // KEY_TILE, B_WIDTH, KEY_TILE), BF16),
                        pltpu.VMEM((B_HEADS, l, LANES), F32)],
        compiler_params=_params(("arbitrary", "arbitrary")),
        name="fox_prompt_attn",
    )(q3, k3, v3, c3, ct4)


BATCH_NT = (((2,), (2,)), ((0,), (0,)))
BATCH_NN = (((2,), (1,)), ((0,), (0,)))


def _attn_sample_kernel(pt_ref, q_ref, kn_ref, vn_ref, lfn_ref, *rest, n_steps, t_new, exact):
    pps = PAGES_PER_STEP
    kp_refs, vp_refs, lp_refs = rest[:pps], rest[pps:2 * pps], rest[2 * pps:3 * pps]
    o_ref = rest[3 * pps]
    m_ref, l_ref, acc_ref, car_ref, bq_ref = rest[3 * pps + 1:]
    j = pl.program_id(1)
    nrow = B_HEADS * t_new
    row_h = lax.broadcasted_iota(jnp.int32, (nrow, 1), 0) // t_new
    row_t = lax.broadcasted_iota(jnp.int32, (nrow, 1), 0) % t_new
    lane = lax.broadcasted_iota(jnp.int32, (1, LANES), 1)
    r_i = lax.broadcasted_iota(jnp.int32, (LANES, LANES), 0)
    c_i = lax.broadcasted_iota(jnp.int32, (LANES, LANES), 1)
    heads = lambda u: jnp.stack([u[:, h * HEAD_DIM:(h + 1) * HEAD_DIM] for h in range(B_HEADS)], axis=0)
    q3 = heads(q_ref[0])

    def to_rows(x8):
        n = x8.shape[-1]
        return jnp.broadcast_to(x8[:, None, :], (B_HEADS, t_new, n)).reshape(nrow, n)

    def scores(k3, keys_minor):
        s = _mm(q3, k3, BATCH_NN if keys_minor else BATCH_NT, exact=exact)
        return s.reshape(nrow, s.shape[2])

    def weighted(p, v3, keys_minor):
        p3 = p.reshape(B_HEADS, t_new, p.shape[1])
        return _mm(p3, v3, BATCH_NT if keys_minor else BATCH_NN, exact=exact).reshape(nrow, HEAD_DIM)

    @pl.when(j == 0)
    def _():
        zrows = lambda u: jnp.concatenate([u, jnp.zeros((LANES - t_new, u.shape[1]), u.dtype)], axis=0)
        lfn = zrows(lfn_ref[0])
        cn = _mm_x((c_i <= r_i).astype(F32), lfn, data="rhs")
        cn_t = cn.T[:B_HEADS, :]
        sel = (lane == row_h).astype(F32)
        cn_rows = jnp.broadcast_to(cn[None, :t_new], (B_HEADS, t_new, LANES)).reshape(nrow, LANES)
        cq = jnp.sum(cn_rows * sel, axis=-1, keepdims=True)
        bq_ref[...] = cq
        s = scores(heads(zrows(kn_ref[0])), False) + cq - to_rows(cn_t)
        s = jnp.where(lane <= row_t, s, NEG_INF)
        m = jnp.max(s, axis=-1, keepdims=True)
        p = jnp.exp(s - m)
        m_ref[...] = m
        l_ref[...] = jnp.sum(p, axis=-1, keepdims=True)
        acc_ref[...] = weighted(p, heads(zrows(vn_ref[0])), False)
        car_ref[...] = jnp.zeros(car_ref.shape, F32)

    cq = bq_ref[...]
    strict = (r_i > c_i).astype(F32)
    m, l, acc, car = m_ref[...], l_ref[...], acc_ref[...], car_ref[...]
    page_scores = []
    for i in range(pps):
        lf_t = lp_refs[i][0, 0]
        suf = _mm_x(lf_t, strict) + car
        car = car + jnp.sum(lf_t, axis=-1, keepdims=True)
        page_scores.append(scores(kp_refs[i][0, 0], True) + cq + to_rows(suf))
    m_new = m
    for s in page_scores:
        m_new = jnp.maximum(m_new, jnp.max(s, axis=-1, keepdims=True))
    alpha = jnp.exp(m - m_new)
    l = alpha * l
    acc = alpha * acc
    for i in range(pps):
        p = jnp.exp(page_scores[i] - m_new)
        l = l + jnp.sum(p, axis=-1, keepdims=True)
        acc = acc + weighted(p, vp_refs[i][0, 0], True)
    m_ref[...], l_ref[...], acc_ref[...], car_ref[...] = m_new, l, acc, car

    @pl.when(j == n_steps - 1)
    def _():
        o = acc / l
        o_ref[0] = jnp.concatenate([o[h * t_new:(h + 1) * t_new, :] for h in range(B_HEADS)], axis=1)


def _attention_sample(q3, k3, v3, lf3, cache_k, cache_v, cache_lf, page_table, exact):
    b, t_new, _ = q3.shape
    n_pages = page_table.shape[1]
    pps = PAGES_PER_STEP
    n_steps = n_pages // pps
    nrow = B_HEADS * t_new

    def page_map(i, n_minor):
        return lambda bi, j, pt: (0, pt[bi, n_pages - 1 - (j * pps + i)]) + (0,) * n_minor

    new = lambda n: pl.BlockSpec((1, t_new, n), lambda bi, j, pt: (bi, 0, 0))
    in_specs = [new(B_WIDTH), new(B_WIDTH), new(B_WIDTH), new(LANES)]
    in_specs += [pl.BlockSpec((1, 1, B_HEADS, HEAD_DIM, PAGE_SIZE), page_map(i, 3)) for i in range(pps)]
    in_specs += [pl.BlockSpec((1, 1, B_HEADS, HEAD_DIM, PAGE_SIZE), page_map(i, 3)) for i in range(pps)]
    in_specs += [pl.BlockSpec((1, 1, B_HEADS, PAGE_SIZE), page_map(i, 2)) for i in range(pps)]
    grid_spec = pltpu.PrefetchScalarGridSpec(
        num_scalar_prefetch=1, grid=(b, n_steps), in_specs=in_specs,
        out_specs=pl.BlockSpec((1, t_new, B_WIDTH), lambda bi, j, pt: (bi, 0, 0)),
        scratch_shapes=[pltpu.VMEM((nrow, 1), F32), pltpu.VMEM((nrow, 1), F32), pltpu.VMEM((nrow, HEAD_DIM), F32),
                        pltpu.VMEM((B_HEADS, 1), F32), pltpu.VMEM((nrow, 1), F32)])
    return pl.pallas_call(
        functools.partial(_attn_sample_kernel, n_steps=n_steps, t_new=t_new, exact=exact),
        grid_spec=grid_spec,
        out_shape=jax.ShapeDtypeStruct((b, t_new, B_WIDTH), F32),
        compiler_params=_params(("arbitrary", "arbitrary")),
        name="fox_sample_attn",
    )(page_table, q3, k3, v3, lf3, *([cache_k] * pps), *([cache_v] * pps), *([cache_lf] * pps))


def _rwkv_kernel(za_ref, s0_ref, sh0_ref, mu_ref, w0_ref, a0_ref, kk_ref, ka_ref, rk_ref, lw_ref, lb_ref,
                 w2_ref, a2_ref, g2_ref, gs_ref, oa_ref, sout_ref, prev_ref, st_ref, *, valid_rows, exact):
    nb, c = za_ref.shape[0], za_ref.shape[1]
    n_pair = A_HEADS // 2
    j = pl.program_id(1)

    @pl.when(j == 0)
    def _():
        for s in range(nb):
            prev_ref[s:s + 1, :] = sh0_ref[s]
            for p in range(n_pair):
                st_ref[s * n_pair + p] = jnp.concatenate([s0_ref[s, 2 * p], s0_ref[s, 2 * p + 1]], axis=-1)

    za = jnp.concatenate([za_ref[s] for s in range(nb)], axis=0)
    row = lax.broadcasted_iota(jnp.int32, (nb * c, 1), 0) % c
    zprev = pltpu.roll(za, 1, 0)
    for s in range(nb):
        seq = lax.broadcasted_iota(jnp.int32, (nb * c, 1), 0) == s * c
        zprev = jnp.where(seq, prev_ref[s:s + 1, :], zprev)
    for s in range(nb):
        prev_ref[s:s + 1, :] = za[(s + 1) * c - 1:(s + 1) * c, :]
    zs = za + (zprev - za) * mu_ref[...]
    i1, i2, i3 = A_WIDTH, 2 * A_WIDTH, 3 * A_WIDTH
    i4 = i3 + W_LORA
    i5 = i4 + A_LORA
    r, k, v = zs[:, :i1], zs[:, i1:i2], zs[:, i2:i3]
    wl, al, gl = zs[:, i3:i4], zs[:, i4:i5], zs[:, i5:]
    logw = -math.exp(-0.5) * _sigmoid(w0_ref[...] + _mm(jnp.tanh(wl), w2_ref[...], exact=exact))
    a = _sigmoid(a0_ref[...] + _mm(al, a2_ref[...], exact=exact))
    g = _mm(_sigmoid(gl), g2_ref[...], exact=exact)
    kk = k * kk_ref[...]
    kk = kk * lax.rsqrt(jnp.maximum(_mm_x(kk * kk, gs_ref[...]), 1e-24))
    kmod = k * (1.0 + (a - 1.0) * ka_ref[...])
    bb = kk * a
    if valid_rows < c:
        live = row < valid_rows
        logw = jnp.where(live, logw, 0.0)
        v, kmod, bb, kk = (jnp.where(live, u, 0.0) for u in (v, kmod, bb, kk))

    ri = lax.broadcasted_iota(jnp.int32, (c, c), 0)
    ci = lax.broadcasted_iota(jnp.int32, (c, c), 1)
    incl = (ci <= ri).astype(F32)
    strict = (ci < ri).astype(F32)
    eye = (ci == ri).astype(F32)
    rb = lax.broadcasted_iota(jnp.int32, (nb * c, nb * c), 0)
    cb = lax.broadcasted_iota(jnp.int32, (nb * c, nb * c), 1)
    seq_incl = ((cb <= rb) & (cb // c == rb // c)).astype(F32)
    gcum = _mm_x(seq_incl, logw, data="rhs")
    glast = jnp.concatenate([jnp.broadcast_to(gcum[(s + 1) * c - 1:(s + 1) * c, :], (c, A_WIDTH)) for s in range(nb)],
                            axis=0)
    p_in, p_out = jnp.exp(gcum), jnp.exp(-gcum)
    tail = jnp.exp(glast - gcum)
    kt = kk * jnp.exp(gcum - logw)
    bt = bb * p_out
    kkt = kmod * p_out
    rt = r * p_in
    bh = bb * tail
    kh = kmod * tail
    pc = jnp.exp(glast)

    pairs = range(nb * n_pair)
    pss = [slice((p % n_pair) * LANES, (p % n_pair + 1) * LANES) for p in pairs]
    rss = [slice((p // n_pair) * c, (p // n_pair + 1) * c) for p in pairs]
    mm = functools.partial(_mm, exact=exact)
    lane = lax.broadcasted_iota(jnp.int32, (1, LANES), 1)
    m_lo = (lane < HEAD_DIM).astype(F32)
    m_hi = 1.0 - m_lo
    bd = lambda t: jnp.concatenate([t * m_lo, t * m_hi], axis=0)
    two = lambda t: jnp.concatenate([t, t], axis=1)
    strict2, incl2, eye2 = two(strict), two(incl), two(eye)

    def head_sum(t):
        s_lo = jnp.sum(t * m_lo, axis=-1, keepdims=True)
        s_hi = jnp.sum(t * m_hi, axis=-1, keepdims=True)
        return s_lo * m_lo + s_hi * m_hi

    m0 = [st_ref[p] for p in pairs]
    vp = [v[rss[p], pss[p]] for p in pairs]
    lhs = [jnp.concatenate([kt[rss[p], pss[p]], rt[rss[p], pss[p]]], axis=0) for p in pairs]
    ab = [mm(lhs[p], bd(bt[rss[p], pss[p]]), NT) for p in pairs]
    ak = [mm(lhs[p], bd(kkt[rss[p], pss[p]]), NT) for p in pairs]
    lm = [mm(lhs[p], bd(m0[p]), NT) for p in pairs]
    pw = [-(ab[p][:c] * strict2) for p in pairs]
    x = [eye2 + pw[p] for p in pairs]
    n = 1
    while 2 * n < c:
        pw = [mm(pw[p], bd(pw[p])) for p in pairs]
        x = [x[p] + mm(x[p], bd(pw[p])) for p in pairs]
        n *= 2
    base = [lm[p][:c] + mm(ak[p][:c] * strict2, bd(vp[p])) for p in pairs]
    u = [mm(x[p], bd(base[p])) for p in pairs]
    o = [lm[p][c:] - mm(ab[p][c:] * incl2, bd(u[p])) + mm(ak[p][c:] * incl2, bd(vp[p])) for p in pairs]
    for p in pairs:
        upd = mm(jnp.concatenate([vp[p], -u[p]], axis=0),
                 jnp.concatenate([kh[rss[p], pss[p]], bh[rss[p], pss[p]]], axis=0), TN)
        row0 = rss[p].start
        st_ref[p] = m0[p] * pc[row0:row0 + 1, pss[p]] + upd[:HEAD_DIM] * m_lo + upd[HEAD_DIM:] * m_hi
    outs = []
    for p in pairs:
        ps, rs = pss[p], rss[p]
        d = o[p] - head_sum(o[p]) * (1.0 / HEAD_DIM)
        var = head_sum(d * d) * (1.0 / HEAD_DIM)
        on = d * lax.rsqrt(var + GN_EPS) * lw_ref[:, ps] + lb_ref[:, ps]
        bonus = head_sum(r[rs, ps] * kmod[rs, ps] * rk_ref[:, ps]) * vp[p]
        outs.append((on + bonus) * g[rs, ps])
    for s in range(nb):
        oa_ref[s] = jnp.concatenate(outs[s * n_pair:(s + 1) * n_pair], axis=-1).astype(oa_ref.dtype)

    @pl.when(j == pl.num_programs(1) - 1)
    def _():
        for p in pairs:
            s, q = divmod(p, n_pair)
            sp = st_ref[p]
            sout_ref[s, 2 * q] = sp[:, :HEAD_DIM]
            sout_ref[s, 2 * q + 1] = sp[:, HEAD_DIM:]


def _rwkv(za3, s0, shift0, w, valid_rows, exact):
    b, l, _ = za3.shape
    c = CHUNK
    nb = math.gcd(b, SEQS_PER_STEP)
    assert c == HEAD_DIM
    vec = lambda n: _full((1, n))
    wdt = F32 if exact else BF16
    return pl.pallas_call(
        functools.partial(_rwkv_kernel, valid_rows=valid_rows, exact=exact),
        grid=(b // nb, l // c),
        in_specs=[pl.BlockSpec((nb, c, A_PROJ), lambda i, j: (i, j, 0)),
                  pl.BlockSpec((nb, A_HEADS, HEAD_DIM, HEAD_DIM), lambda i, j: (i, 0, 0, 0)),
                  pl.BlockSpec((nb, 1, A_PROJ), lambda i, j: (i, 0, 0)),
                  vec(A_PROJ), vec(A_WIDTH), vec(A_WIDTH), vec(A_WIDTH), vec(A_WIDTH), vec(A_WIDTH),
                  vec(A_WIDTH), vec(A_WIDTH),
                  _full((W_LORA, A_WIDTH)), _full((A_LORA, A_WIDTH)), _full((G_LORA, A_WIDTH)),
                  _full((A_WIDTH, A_WIDTH))],
        out_specs=[pl.BlockSpec((nb, c, A_WIDTH), lambda i, j: (i, j, 0)),
                   pl.BlockSpec((nb, A_HEADS, HEAD_DIM, HEAD_DIM), lambda i, j: (i, 0, 0, 0))],
        out_shape=[jax.ShapeDtypeStruct((b, l, A_WIDTH), F32 if exact else BF16),
                   jax.ShapeDtypeStruct((b, A_HEADS, HEAD_DIM, HEAD_DIM), F32)],
        scratch_shapes=[pltpu.VMEM((nb, A_PROJ), F32), pltpu.VMEM((nb * A_HEADS // 2, HEAD_DIM, LANES), F32)],
        compiler_params=_params(("arbitrary", "arbitrary")),
        name="rwkv7_chunked",
    )(za3, s0, shift0[:, None, :], w['mu'], w['w0'], w['a0'], w['k_k'], w['k_a'], w['r_k'], w['lnx_w'], w['lnx_b'],
      w['w2'].astype(wdt), w['a2'].astype(wdt), w['g2'].astype(wdt), w['gsum'])


def _merge_kernel(oa_ref, ob_ref, sg_ref, x_ref, pa_ref, pb_ref, wo_ref, ln2_ref, wr_ref,
                  h_ref, xn_ref, lg_ref, *, exact):
    sg = sg_ref[0].astype(F32)
    ya = _mm(oa_ref[0], pa_ref[...], exact=exact)
    yb = _mm(ob_ref[0], pb_ref[...], exact=exact)
    merged = sg[:, :D_MODEL] * ya + sg[:, D_MODEL:] * yb
    h = x_ref[0] + _mm(merged, wo_ref[...], exact=exact)
    h_ref[0] = h
    xn = h * lax.rsqrt(jnp.mean(h * h, axis=-1, keepdims=True) + RMS_EPS) * ln2_ref[...]
    xn_ref[0] = xn.astype(xn_ref.dtype)
    lg_ref[0] = _mm(xn, wr_ref[...], exact=exact)


def _merge(oa3, ob3, sg3, x3, w, row_off, exact, tm):
    b, s, _ = x3.shape
    wdt = F32 if exact else BF16
    blk = lambda n, off: pl.BlockSpec((1, tm, n), lambda i, j: (i, j + off, 0))
    return pl.pallas_call(
        functools.partial(_merge_kernel, exact=exact),
        grid=(b, s // tm),
        in_specs=[blk(A_WIDTH, row_off), blk(B_WIDTH, row_off), blk(2 * D_MODEL, row_off), blk(D_MODEL, 0),
                  _full((A_WIDTH, D_MODEL)), _full((B_WIDTH, D_MODEL)), _full((D_MODEL, D_MODEL)),
                  _full((1, D_MODEL)), _full((D_MODEL, LANES))],
        out_specs=[blk(D_MODEL, 0), blk(D_MODEL, 0), blk(LANES, 0)],
        out_shape=[jax.ShapeDtypeStruct((b, s, D_MODEL), F32),
                   jax.ShapeDtypeStruct((b, s, D_MODEL), F32),
                   jax.ShapeDtypeStruct((b, s, LANES), F32)],
        compiler_params=_params(("arbitrary", "arbitrary")),
        name="merge_out",
    )(oa3, ob3, sg3, x3, w['proj_a'].astype(wdt), w['proj_b'].astype(wdt), w['w_out'].astype(wdt),
      w['ln2'], w['wr'].astype(wdt))


def _route_parts(lg):
    lane = lax.broadcasted_iota(jnp.int32, lg.shape, 1)
    big = jnp.int32(1 << 20)
    is_g = (lane >= N_EXPERTS) & (lane < N_EXPERTS + N_GROUPS)
    gl = jnp.where(is_g, lg, NEG_INF)
    gmax = jnp.max(gl, axis=-1, keepdims=True)
    gp = 1.0 / jnp.sum(jnp.where(is_g, jnp.exp(gl - gmax), 0.0), axis=-1, keepdims=True)
    gi = jnp.min(jnp.where(is_g & (gl == gmax), lane, big), axis=-1, keepdims=True) - N_EXPERTS
    in_g = (lane < N_EXPERTS) & (lane // EXP_PER_GROUP == gi)
    sel = jnp.where(in_g, lg, NEG_INF)
    v1 = jnp.max(sel, axis=-1, keepdims=True)
    i1 = jnp.min(jnp.where(in_g & (sel == v1), lane, big), axis=-1, keepdims=True)
    sel2 = jnp.where(lane == i1, NEG_INF, sel)
    v2 = jnp.max(sel2, axis=-1, keepdims=True)
    i2 = jnp.min(jnp.where(in_g & (lane != i1) & (sel2 == v2), lane, big), axis=-1, keepdims=True)
    e2 = jnp.exp(v2 - v1)
    w1 = gp / (1.0 + e2)
    w2 = gp * e2 / (1.0 + e2)
    return i1, i2, w1, w2, lane


def _route(lg):
    i1, i2, w1, w2, lane = _route_parts(lg)
    return jnp.where(lane == i1, w1, 0.0) + jnp.where(lane == i2, w2, 0.0)


def _route_kernel(lg_ref, mi_ref, mw_ref, cnt_ref, car_ref):
    @pl.when(pl.program_id(0) == 0)
    def _():
        car_ref[...] = jnp.zeros(car_ref.shape, F32)

    i1, i2, w1, w2, lane = _route_parts(lg_ref[...])
    tm = lg_ref.shape[0]
    oh1 = (lane == i1).astype(F32)
    oh2 = (lane == i2).astype(F32)
    oh = oh1 + oh2
    r = lax.broadcasted_iota(jnp.int32, (tm, tm), 0)
    c = lax.broadcasted_iota(jnp.int32, (tm, tm), 1)
    seen = _mm((c < r).astype(F32), oh) + car_ref[...]
    rank1 = jnp.sum(oh1 * seen, axis=-1, keepdims=True).astype(jnp.int32)
    rank2 = jnp.sum(oh2 * seen, axis=-1, keepdims=True).astype(jnp.int32)
    car_ref[...] = seen[tm - 1:tm] + oh[tm - 1:tm]
    cnt_ref[...] = car_ref[...]
    zero = jnp.zeros(lane.shape, jnp.int32)
    mi_ref[...] = jnp.where(lane == 0, i1, jnp.where(lane == 1, i2, jnp.where(lane == 2, rank1,
                                                                               jnp.where(lane == 3, rank2, zero))))
    mw_ref[...] = jnp.where(lane == 0, w1, jnp.where(lane == 1, w2, 0.0))


def _moe_route(lg2d, tm):
    t = lg2d.shape[0]
    row = lambda: pl.BlockSpec((tm, LANES), lambda i: (i, 0))
    return pl.pallas_call(
        _route_kernel,
        grid=(t // tm,),
        in_specs=[row()],
        out_specs=[row(), row(), pl.BlockSpec((1, LANES), lambda i: (0, 0))],
        out_shape=[jax.ShapeDtypeStruct((t, LANES), jnp.int32), jax.ShapeDtypeStruct((t, LANES), F32),
                   jax.ShapeDtypeStruct((1, LANES), F32)],
        scratch_shapes=[pltpu.VMEM((1, LANES), F32)],
        compiler_params=_params(("arbitrary",)),
        name="moe_route",
    )(lg2d)


def _row_copy(src_ref, src_row, dst_ref, dst_row, sem):
    return pltpu.make_async_copy(src_ref.at[pl.ds(src_row, 1)], dst_ref.at[pl.ds(dst_row, 1)], sem)


def _dispatch_kernel(x_ref, pos_ref, xs_in_ref, xs_ref, sem):
    del xs_in_ref
    tm = x_ref.shape[0]

    def issue(r, carry):
        for s in range(2):
            _row_copy(x_ref, r, xs_ref, pos_ref[0, 0, s * tm + r], sem).start(priority=s)
        return carry

    def drain(r, carry):
        for s in range(2):
            _row_copy(x_ref, 0, xs_ref, 0, sem).wait()
        return carry

    lax.fori_loop(0, tm, issue, 0, unroll=8)
    lax.fori_loop(0, tm, drain, 0, unroll=8)


def _moe_dispatch(xn2d, pos_tiles, n_rows, tm):
    t = xn2d.shape[0]
    xs0 = jnp.zeros((n_rows, D_MODEL), xn2d.dtype)
    return pl.pallas_call(
        _dispatch_kernel,
        grid=(t // tm,),
        in_specs=[pl.BlockSpec((tm, D_MODEL), lambda i: (i, 0)),
                  pl.BlockSpec((1, 1, 2 * tm), lambda i: (i, 0, 0), memory_space=pltpu.SMEM),
                  pl.BlockSpec(memory_space=pl.ANY)],
        out_specs=pl.BlockSpec(memory_space=pl.ANY),
        out_shape=jax.ShapeDtypeStruct((n_rows, D_MODEL), xn2d.dtype),
        scratch_shapes=[pltpu.SemaphoreType.DMA],
        input_output_aliases={2: 0},
        compiler_params=_params(("arbitrary",)),
        name="moe_dispatch",
    )(xn2d, pos_tiles, xs0)


def _ffn_kernel(te_ref, tv_ref, xs_ref, w1_ref, w3_ref, w2_ref, ys_ref):
    del te_ref
    live = tv_ref[pl.program_id(0)] == 1

    @pl.when(live)
    def _():
        x = xs_ref[...]
        a = _mm(x, w1_ref[0])
        hid = a * _sigmoid(a) * _mm(x, w3_ref[0])
        ys_ref[...] = _mm(hid, w2_ref[0])

    @pl.when(jnp.logical_not(live))
    def _():
        ys_ref[...] = jnp.zeros(ys_ref.shape, F32)


def _moe_ffn(xs, tile_e, tile_v, w, tm):
    n_rows = xs.shape[0]
    wsel = lambda shape: pl.BlockSpec((1,) + shape, lambda i, te, tv: (te[i], 0, 0))
    grid_spec = pltpu.PrefetchScalarGridSpec(
        num_scalar_prefetch=2, grid=(n_rows // tm,),
        in_specs=[pl.BlockSpec((tm, D_MODEL), lambda i, te, tv: (i, 0)),
                  wsel((D_MODEL, EXPERT_FF)), wsel((D_MODEL, EXPERT_FF)), wsel((EXPERT_FF, D_MODEL))],
        out_specs=pl.BlockSpec((tm, D_MODEL), lambda i, te, tv: (i, 0)))
    return pl.pallas_call(
        _ffn_kernel, grid_spec=grid_spec,
        out_shape=jax.ShapeDtypeStruct((n_rows, D_MODEL), F32),
        compiler_params=_params(("arbitrary",)),
        name="moe_ffn",
    )(tile_e, tile_v, xs, w['w1'].astype(BF16), w['w3'].astype(BF16), w['w2'].astype(BF16))


def _combine_kernel(h_ref, mw_ref, pos_ref, ys_ref, lnf_ref, y_ref, buf_ref, sem):
    tm = h_ref.shape[0]

    def issue(r, carry):
        for s in range(2):
            _row_copy(ys_ref, pos_ref[0, 0, s * tm + r], buf_ref.at[s], r, sem).start(priority=s)
        return carry

    def drain(r, carry):
        for s in range(2):
            _row_copy(ys_ref, 0, buf_ref.at[s], 0, sem).wait()
        return carry

    lax.fori_loop(0, tm, issue, 0, unroll=8)
    lax.fori_loop(0, tm, drain, 0, unroll=8)
    lane = lax.broadcasted_iota(jnp.int32, (1, LANES), 1)
    mw = mw_ref[...]
    wa = jnp.sum(jnp.where(lane == 0, mw, 0.0), axis=-1, keepdims=True)
    wb = jnp.sum(jnp.where(lane == 1, mw, 0.0), axis=-1, keepdims=True)
    y = h_ref[...] + (wa * buf_ref[0] + wb * buf_ref[1])
    y_ref[...] = y * lax.rsqrt(jnp.mean(y * y, axis=-1, keepdims=True) + RMS_EPS) * lnf_ref[...]


def _moe_combine(h2d, mw, pos_tiles, ys, lnf, tm):
    t = h2d.shape[0]
    row = lambda n: pl.BlockSpec((tm, n), lambda i: (i, 0))
    return pl.pallas_call(
        _combine_kernel,
        grid=(t // tm,),
        in_specs=[row(D_MODEL), row(LANES),
                  pl.BlockSpec((1, 1, 2 * tm), lambda i: (i, 0, 0), memory_space=pltpu.SMEM),
                  pl.BlockSpec(memory_space=pl.ANY), pl.BlockSpec((1, D_MODEL), lambda i: (0, 0))],
        out_specs=row(D_MODEL),
        out_shape=jax.ShapeDtypeStruct((t, D_MODEL), F32),
        scratch_shapes=[pltpu.VMEM((2, tm, D_MODEL), F32), pltpu.SemaphoreType.DMA],
        compiler_params=_params(("arbitrary",)),
        name="moe_combine",
    )(h2d, mw, pos_tiles, ys, lnf)


def _moe_routed(xn2d, lg2d, h2d, w):
    t = xn2d.shape[0]
    tm = ROUTE_TILE
    mi, mw, cnt = _moe_route(lg2d, math.gcd(t, 512))
    counts = cnt[0, :N_EXPERTS].astype(jnp.int32)
    padded = (counts + tm - 1) // tm * tm
    ends = jnp.cumsum(padded)
    offs = ends - padded
    n_rows = (2 * t + tm - 1) // tm * tm + N_EXPERTS * tm
    starts = jnp.arange(n_rows // tm, dtype=jnp.int32) * tm
    tile_e = jnp.sum(starts[:, None] >= ends[None, :], axis=1).astype(jnp.int32)
    tile_v = (tile_e < N_EXPERTS).astype(jnp.int32)
    tile_e = jnp.minimum(tile_e, N_EXPERTS - 1)
    td = math.gcd(t, 256)
    pos = jnp.stack([offs[mi[:, s]] + mi[:, 2 + s] for s in range(2)])
    pos_tiles = pos.reshape(2, t // td, td).transpose(1, 0, 2).reshape(t // td, 1, 2 * td)
    xs = _moe_dispatch(xn2d, pos_tiles, n_rows, td)
    ys = _moe_ffn(xs, tile_e, tile_v, w, tm)
    return _moe_combine(h2d, mw, pos_tiles, ys, w['lnf'], td)


def _moe_kernel(xn_ref, lg_ref, h_ref, w1_ref, w3_ref, w2_ref, lnf_ref, y_ref, acc_ref, cmb_ref, *, exact):
    e = pl.program_id(1)

    @pl.when(e == 0)
    def _():
        acc_ref[...] = jnp.zeros(acc_ref.shape, F32)
        cmb_ref[...] = _route(lg_ref[...])

    x = xn_ref[...]
    a = _mm(x, w1_ref[0], exact=exact)
    hid = a * _sigmoid(a) * _mm(x, w3_ref[0], exact=exact)
    lane = lax.broadcasted_iota(jnp.int32, (1, LANES), 1)
    ce = jnp.sum(jnp.where(lane == e, cmb_ref[...], 0.0), axis=-1, keepdims=True)
    acc_ref[...] += _mm(hid * ce, w2_ref[0], exact=exact)

    @pl.when(e == N_EXPERTS - 1)
    def _():
        y = h_ref[...] + acc_ref[...]
        y_ref[...] = y * lax.rsqrt(jnp.mean(y * y, axis=-1, keepdims=True) + RMS_EPS) * lnf_ref[...]


def _moe(xn2d, lg2d, h2d, w, exact, tm):
    t = xn2d.shape[0]
    wdt = F32 if exact else BF16
    row = lambda n: pl.BlockSpec((tm, n), lambda i, e: (i, 0))
    return pl.pallas_call(
        functools.partial(_moe_kernel, exact=exact),
        grid=(t // tm, N_EXPERTS),
        in_specs=[row(D_MODEL), row(LANES), row(D_MODEL),
                  pl.BlockSpec((1, D_MODEL, EXPERT_FF), lambda i, e: (e, 0, 0)),
                  pl.BlockSpec((1, D_MODEL, EXPERT_FF), lambda i, e: (e, 0, 0)),
                  pl.BlockSpec((1, EXPERT_FF, D_MODEL), lambda i, e: (e, 0, 0)),
                  pl.BlockSpec((1, D_MODEL), lambda i, e: (0, 0))],
        out_specs=row(D_MODEL),
        out_shape=jax.ShapeDtypeStruct((t, D_MODEL), F32),
        scratch_shapes=[pltpu.VMEM((tm, D_MODEL), F32), pltpu.VMEM((tm, LANES), F32)],
        compiler_params=_params(("arbitrary", "arbitrary")),
        name="hier_moe",
    )(xn2d, lg2d, h2d, w['w1'].astype(wdt), w['w3'].astype(wdt), w['w2'].astype(wdt), w['lnf'])


def _layer_weights(l, ln1_w, w_in, rwkv_mu, rwkv_w0, rwkv_w2, rwkv_a0, rwkv_a2, rwkv_g2, rwkv_k_k, rwkv_k_a,
                   rwkv_r_k, rwkv_lnx_w, rwkv_lnx_b, fox_q_norm, fox_k_norm, fox_f_bias, proj_a, proj_b, w_out,
                   ln2_w, router_grp, router_exp, exp_w1, exp_w3, exp_w2, ln_f):
    wi = w_in[l]
    c1 = A_PROJ
    c2 = c1 + 3 * B_WIDTH
    c3 = c2 + B_HEADS
    head_of = jnp.arange(B_WIDTH) // HEAD_DIM
    same_head = (head_of[:, None] == head_of[None, :]).astype(F32)
    row = lambda u: u.reshape(1, -1).astype(F32)
    return dict(
        ln1=row(ln1_w[l]), wa=wi[:, :c1], wqkv=wi[:, c1:c2],
        wf=jnp.pad(wi[:, c2:c3], ((0, 0), (0, LANES - B_HEADS))), wg=wi[:, c3:],
        fb=jnp.pad(row(fox_f_bias[l]), ((0, 0), (0, LANES - B_HEADS))),
        qn=row(jnp.tile(fox_q_norm[l], B_HEADS)), kn=row(jnp.tile(fox_k_norm[l], B_HEADS)),
        gmean=same_head / HEAD_DIM, gsum=same_head,
        mu=row(rwkv_mu[l]), w0=row(rwkv_w0[l]), a0=row(rwkv_a0[l]), k_k=row(rwkv_k_k[l]), k_a=row(rwkv_k_a[l]),
        r_k=row(rwkv_r_k[l]), lnx_w=row(rwkv_lnx_w[l]), lnx_b=row(rwkv_lnx_b[l]),
        w2=rwkv_w2[l], a2=rwkv_a2[l], g2=rwkv_g2[l],
        proj_a=proj_a[l], proj_b=proj_b[l], w_out=w_out[l], ln2=row(ln2_w[l]),
        wr=jnp.pad(jnp.concatenate([router_exp[l], router_grp[l]], axis=1),
                   ((0, 0), (0, LANES - N_EXPERTS - N_GROUPS))),
        w1=exp_w1[l], w3=exp_w3[l], w2e=exp_w2[l], lnf=row(ln_f))


def kernel(x_prompt, x_sample, cache_k, cache_v, cache_logf, state_rwkv, state_shift, page_table,
           meta_tokens, ln1_w, w_in, rwkv_mu, rwkv_w0, rwkv_w2, rwkv_a0, rwkv_a2, rwkv_g2,
           rwkv_k_k, rwkv_k_a, rwkv_r_k, rwkv_lnx_w, rwkv_lnx_b, fox_q_norm, fox_k_norm, fox_f_bias,
           proj_a, proj_b, w_out, ln2_w, router_grp, router_exp, exp_w1, exp_w3, exp_w2, ln_f):
    depth = w_in.shape[0]
    assert depth == 1, "single trunk layer"
    b_p, seq, _ = x_prompt.shape
    b_s, t_new, _ = x_sample.shape
    assert seq % ROW_TILE == 0 and t_new <= CHUNK and page_table.shape[1] % PAGES_PER_STEP == 0
    w = _layer_weights(0, ln1_w, w_in, rwkv_mu, rwkv_w0, rwkv_w2, rwkv_a0, rwkv_a2, rwkv_g2, rwkv_k_k, rwkv_k_a,
                       rwkv_r_k, rwkv_lnx_w, rwkv_lnx_b, fox_q_norm, fox_k_norm, fox_f_bias, proj_a, proj_b,
                       w_out, ln2_w, router_grp, router_exp, exp_w1, exp_w3, exp_w2, ln_f)
    wm = dict(w, w2=w['w2e'])

    pad_front = ROW_TILE - N_META
    l_pad = ROW_TILE + seq
    l_real = N_META + seq
    head = jnp.concatenate([jnp.zeros((pad_front, D_MODEL), F32), meta_tokens.astype(F32)], axis=0)
    za, q, k, v, lf, sg = _project(x_prompt, w, False, ROW_TILE, head=head)
    r3 = lambda u: u.reshape(b_p, l_pad, u.shape[-1])
    c3, ct4 = _cumsum(r3(lf))
    ob = _attention_prompt(r3(q), r3(k), r3(v), c3, ct4, pad_front)
    oa, s_p = _rwkv(r3(za), jnp.zeros((b_p, A_HEADS, HEAD_DIM, HEAD_DIM), F32), jnp.zeros((b_p, A_PROJ), F32),
                    w, CHUNK, False)
    h_p, xn_p, lg_p = _merge(oa, ob, r3(sg), x_prompt, w, 1, False, ROW_TILE)
    flat = lambda u: u.reshape(b_p * seq, u.shape[-1])
    y_prompt = _moe_routed(flat(xn_p), flat(lg_p), flat(h_p), wm).reshape(b_p, seq, D_MODEL)
    k_p = r3(k)[:, pad_front:].reshape(1, b_p, l_real, B_HEADS, HEAD_DIM)
    v_p = r3(v)[:, pad_front:].reshape(1, b_p, l_real, B_HEADS, HEAD_DIM)
    lf_p = r3(lf)[:, pad_front:, :B_HEADS][None]
    sh_p = r3(za)[:, l_pad - 1][None]

    n_s = b_s * t_new
    za_s, q_s, k_s, v_s, lf_s, sg_s = _project(x_sample.reshape(n_s, D_MODEL), w, True, n_s)
    s3 = lambda u: u.reshape(b_s, t_new, u.shape[-1])
    keys_minor = lambda u: jnp.moveaxis(u, 2, -1)
    ob_s = _attention_sample(s3(q_s), s3(k_s), s3(v_s), s3(lf_s), keys_minor(cache_k), keys_minor(cache_v),
                             keys_minor(cache_logf), page_table, False)
    za_pad = jnp.pad(s3(za_s), ((0, 0), (0, CHUNK - t_new), (0, 0)))
    oa_s, s_s = _rwkv(za_pad, state_rwkv[0], state_shift[0], w, t_new, True)
    one = lambda u: u.reshape(1, n_s, u.shape[-1])
    h_s, xn_s, lg_s = _merge(one(oa_s[:, :t_new]), one(ob_s), one(sg_s), one(x_sample), w, 0, True, n_s)
    y_sample = _moe(xn_s[0], lg_s[0], h_s[0], wm, True, n_s).reshape(b_s, t_new, D_MODEL)
    k_sn = s3(k_s).reshape(1, b_s, t_new, B_HEADS, HEAD_DIM)
    v_sn = s3(v_s).reshape(1, b_s, t_new, B_HEADS, HEAD_DIM)
    lf_sn = s3(lf_s)[:, :, :B_HEADS][None]
    sh_s = s3(za_s)[:, t_new - 1][None]
    return (y_prompt, y_sample, k_p, v_p, lf_p, s_p[None], sh_p, k_sn, v_sn, lf_sn, s_s[None], sh_s)
```
